```python
import jax
import jax.numpy as jnp
from jax import lax
import numpy as np

D_MODEL = 1024
BATCH = 16
SEQ = 2048
DEPTH = 4

GRID_W = 64
CTX_LEN = 256
HG_HEADS = 4
HG_HEAD_DIM = 128
HG_WIDTH = HG_HEADS * HG_HEAD_DIM
DN_HEADS = 4
DN_HEAD_DIM = 128
DN_WIDTH = DN_HEADS * DN_HEAD_DIM
MIX_WIDTH = HG_WIDTH + DN_WIDTH
HG_CHUNK = 32
DN_CHUNK = 64
CONV_SIZE = 3
N_GROUPS = 4
EXPERTS_PER_GROUP = 4
N_EXPERTS = N_GROUPS * EXPERTS_PER_GROUP
TOP_K_IN_GROUP = 2
EXPERT_FF = 512
NORM_EPS = 1e-6
IN_SPLITS = [HG_WIDTH] * 5 + [DN_WIDTH] * 4 + [DN_HEADS] * 4
IN_COLS = sum(IN_SPLITS)

kernel_name = 'hybrid_hgrn2_gdn_hmoe_dit'


def rms_norm(x, w):
    x32 = x.astype(jnp.float32)
    y = x32 * lax.rsqrt(jnp.mean(x32 * x32, axis=-1, keepdims=True) + NORM_EPS)
    return (y * w.astype(jnp.float32)).astype(x.dtype)


def modulate(h, shift, scale):
    return h * (1 + scale) + shift


def l2_normalize(t):
    t32 = t.astype(jnp.float32)
    return (t32 * lax.rsqrt(jnp.sum(t32 * t32, axis=-1, keepdims=True) + NORM_EPS)).astype(t.dtype)


def to_heads(t, n_heads):
    b, length, width = t.shape
    return t.reshape(b, length, n_heads, width // n_heads).transpose(0, 2, 1, 3)


def gated_head_norm(o, gate, w):
    y = rms_norm(o, w) * jax.nn.silu(gate)
    b, h, length, d = y.shape
    return y.transpose(0, 2, 1, 3).reshape(b, length, h * d)


def split_in_proj(p):
    bounds = [int(v) for v in np.cumsum(IN_SPLITS)[:-1]]
    return jnp.split(p, bounds, axis=-1)


def gla_chunk_scan(q, k, v, log_f, s0):
    b, h, length, dk = q.shape
    dv = v.shape[-1]
    n = length // HG_CHUNK
    incl = jnp.tril(jnp.ones((HG_CHUNK, HG_CHUNK), dtype=bool))[:, :, None]

    def to_chunks(t):
        return jnp.moveaxis(t.reshape(b, h, n, HG_CHUNK, t.shape[-1]), 2, 0)

    def step(state, inp):
        qc, kc, vc, fc = inp
        cum = jnp.cumsum(fc, axis=2)
        diff = cum[:, :, :, None, :] - cum[:, :, None, :, :]
        decay = jnp.where(incl, jnp.exp(jnp.where(incl, diff, 0.0)), 0.0)
        attn = jnp.einsum('bhtd,bhsd,bhtsd->bhts', qc, kc, decay)
        o = jnp.einsum('bhtd,bhde->bhte', qc * jnp.exp(cum), state) + jnp.einsum('bhts,bhse->bhte', attn, vc)
        last = cum[:, :, -1, :]
        k_dec = kc * jnp.exp(last[:, :, None, :] - cum)
        state = jnp.exp(last)[..., None] * state + jnp.einsum('bhsd,bhse->bhde', k_dec, vc)
        return state, o

    state, o = lax.scan(step, s0, (to_chunks(q), to_chunks(k), to_chunks(v), to_chunks(log_f)))
    return jnp.moveaxis(o, 0, 2).reshape(b, h, length, dv), state


def gated_delta_chunk_scan(q, k, v, beta, log_a, s0):
    b, h, length, dk = q.shape
    dv = v.shape[-1]
    n = length // DN_CHUNK
    dt = q.dtype
    qn = q.reshape(b, h, n, DN_CHUNK, dk)
    kn = k.reshape(b, h, n, DN_CHUNK, dk)
    vn = v.reshape(b, h, n, DN_CHUNK, dv)
    bn = beta.reshape(b, h, n, DN_CHUNK).astype(jnp.float32)
    cum = jnp.cumsum(log_a.astype(jnp.float32).reshape(b, h, n, DN_CHUNK), axis=-1)
    incl = jnp.tril(jnp.ones((DN_CHUNK, DN_CHUNK), dtype=bool))
    strict = jnp.tril(jnp.ones((DN_CHUNK, DN_CHUNK), dtype=bool), -1)
    diff = cum[..., :, None] - cum[..., None, :]
    decay = jnp.where(incl, jnp.exp(jnp.where(incl, diff, 0.0)), 0.0)
    kk = jnp.einsum('bhntd,bhnsd->bhnts', kn, kn).astype(jnp.float32)
    a_mat = jnp.where(strict, bn[..., :, None] * kk * decay, 0.0) + jnp.eye(DN_CHUNK, dtype=jnp.float32)
    rhs = jnp.concatenate([kn.astype(jnp.float32) * (bn * jnp.exp(cum))[..., None],
                           vn.astype(jnp.float32) * bn[..., None]], axis=-1)
    sol = lax.linalg.triangular_solve(a_mat, rhs, left_side=True, lower=True, unit_diagonal=True)
    w = sol[..., :dk].astype(dt)
    u = sol[..., dk:].astype(dt)
    qk = jnp.einsum('bhntd,bhnsd->bhnts', qn, kn) * decay.astype(dt)
    q_dec = qn * jnp.exp(cum)[..., None].astype(dt)
    k_dec = kn * jnp.exp(cum[..., -1:] - cum)[..., None].astype(dt)
    g_tot = jnp.exp(cum[..., -1]).astype(dt)

    def step(state, inp):
        w_c, u_c, qk_c, qd_c, kd_c, gt_c = inp
        v_new = u_c - jnp.einsum('bhtd,bhde->bhte', w_c, state)
        o = jnp.einsum('bhtd,bhde->bhte', qd_c, state) + jnp.einsum('bhts,bhse->bhte', qk_c, v_new)
        state = gt_c[..., None, None] * state + jnp.einsum('bhtd,bhte->bhde', kd_c, v_new)
        return state, o

    xs = tuple(jnp.moveaxis(t, 2, 0) for t in (w, u, qk, q_dec, k_dec, g_tot))
    state, o = lax.scan(step, s0, xs)
    return jnp.moveaxis(o, 0, 2).reshape(b, h, length, dv), state


def run_bidirectional(scan_fn, ctx_fwd, lat_fwd, ctx_bwd, lat_bwd, s0):
    oc_f, sc_f = scan_fn(*ctx_fwd, s0)
    ox_f, _ = scan_fn(*lat_fwd, sc_f)
    oc_b, sc_b = scan_fn(*[jnp.flip(t, axis=2) for t in ctx_bwd], s0)
    ox_b, _ = scan_fn(*[jnp.flip(t, axis=2) for t in lat_bwd], sc_b)
    return oc_f + jnp.flip(oc_b, axis=2), ox_f + jnp.flip(ox_b, axis=2)


def hgrn2_log_forget(f_logit, lb):
    return jnp.logaddexp(jnp.log(lb), jnp.log1p(-lb) + jax.nn.log_sigmoid(f_logit))


def hgrn2_branch(cols_c, cols_x, lb, norm_w):
    def prep(cols):
        q, i, g, f_fwd, f_bwd = cols
        q = to_heads(jax.nn.silu(q), HG_HEADS)
        v = to_heads(i, HG_HEADS)
        per_dir = []
        for f_logit, lb_d in ((f_fwd, lb[0]), (f_bwd, lb[1])):
            log_f = to_heads(hgrn2_log_forget(f_logit, lb_d), HG_HEADS)
            per_dir.append((q, -jnp.expm1(log_f), v, log_f))
        return per_dir, to_heads(g, HG_HEADS)

    dirs_c, gc = prep(cols_c)
    dirs_x, gx = prep(cols_x)
    s0 = jnp.zeros((gc.shape[0], HG_HEADS, HG_HEAD_DIM, HG_HEAD_DIM), gc.dtype)
    oc, ox = run_bidirectional(gla_chunk_scan, dirs_c[0], dirs_x[0], dirs_c[1], dirs_x[1], s0)
    return gated_head_norm(oc, gc, norm_w), gated_head_norm(ox, gx, norm_w)


def short_conv_latent(t, w):
    b, length, ch = t.shape
    rows = length // GRID_W
    y = lax.conv_general_dilated(t.reshape(b, rows, GRID_W, ch), w[:, :, None, :], window_strides=(1, 1),
                                 padding='SAME', dimension_numbers=('NHWC', 'HWIO', 'NHWC'),
                                 feature_group_count=ch)
    return y.reshape(b, length, ch)


def short_conv_context(t, w):
    ch = t.shape[-1]
    return lax.conv_general_dilated(t, w[CONV_SIZE // 2][:, None, :], window_strides=(1,), padding='SAME',
                                    dimension_numbers=('NWC', 'WIO', 'NWC'), feature_group_count=ch)


def deltanet_branch(cols_c, cols_x, conv_w, a_log, dt_bias, norm_w):
    def prep(cols, conv_fn):
        q, k, v, z, b_fwd, b_bwd, a_fwd, a_bwd = cols
        qkv = jax.nn.silu(conv_fn(jnp.concatenate([q, k, v], axis=-1), conv_w))
        q, k, v = jnp.split(qkv, 3, axis=-1)
        q = l2_normalize(to_heads(q, DN_HEADS)) * (DN_HEAD_DIM ** -0.5)
        k = l2_normalize(to_heads(k, DN_HEADS))
        v = to_heads(v, DN_HEADS)
        per_dir = []
        for d, (b_logit, a_logit) in enumerate(((b_fwd, a_fwd), (b_bwd, a_bwd))):
            beta = jax.nn.sigmoid(b_logit).transpose(0, 2, 1)
            log_a = (-jnp.exp(a_log[d]) * jax.nn.softplus(a_logit + dt_bias[d])).transpose(0, 2, 1)
            per_dir.append((q, k, v, beta, log_a))
        return per_dir, to_heads(z, DN_HEADS)

    dirs_c, zc = prep(cols_c, short_conv_context)
    dirs_x, zx = prep(cols_x, short_conv_latent)
    s0 = jnp.zeros((zc.shape[0], DN_HEADS, DN_HEAD_DIM, DN_HEAD_DIM), zc.dtype)
    oc, ox = run_bidirectional(gated_delta_chunk_scan, dirs_c[0], dirs_x[0], dirs_c[1], dirs_x[1], s0)
    return gated_head_norm(oc, zc, norm_w), gated_head_norm(ox, zx, norm_w)


def hybrid_mixer(pc, px, lb, hg_norm_w, conv_w, a_log, dt_bias, dn_norm_w):
    cols_c = split_in_proj(pc)
    cols_x = split_in_proj(px)
    hg_c, hg_x = hgrn2_branch(cols_c[:5], cols_x[:5], lb, hg_norm_w)
    dn_c, dn_x = deltanet_branch(cols_c[5:], cols_x[5:], conv_w, a_log, dt_bias, dn_norm_w)
    return jnp.concatenate([hg_c, dn_c], axis=-1), jnp.concatenate([hg_x, dn_x], axis=-1)


def hierarchical_moe(h, w_group, b_group, w_expert, b_expert, w_gate, w_up, w_down):
    t = h.reshape(-1, h.shape[-1])
    g_logits = (t @ w_group + b_group).astype(jnp.float32)
    g_prob = jax.nn.softmax(g_logits, axis=-1)
    g_idx = jnp.argmax(g_logits, axis=-1)
    g_w = jnp.take_along_axis(g_prob, g_idx[:, None], axis=-1)
    e_logits = (t @ w_expert + b_expert).astype(jnp.float32).reshape(-1, N_GROUPS, EXPERTS_PER_GROUP)
    e_in_group = jnp.take_along_axis(e_logits, g_idx[:, None, None], axis=1)[:, 0]
    top_v, top_i = lax.top_k(e_in_group, TOP_K_IN_GROUP)
    top_w = jax.nn.softmax(top_v, axis=-1) * g_w
    expert_id = g_idx[:, None] * EXPERTS_PER_GROUP + top_i
    gates = jnp.sum(jax.nn.one_hot(expert_id, N_EXPERTS, dtype=jnp.float32) * top_w[..., None],
                    axis=1).astype(t.dtype)
    y = jnp.zeros_like(t)
    for e in range(N_EXPERTS):
        a = jax.nn.silu(t @ w_gate[e]) * (t @ w_up[e])
        y = y + gates[:, e:e + 1] * (a @ w_down[e])
    return y.reshape(h.shape)


def setup_inputs(seed: int = 0) -> dict:
    key = jax.random.key(seed)
    ks = jax.random.split(key, 24)
    f32 = jnp.float32

    def nrm(k, shape, scale):
        return jax.random.normal(k, shape, f32) * scale

    a_init = jax.random.uniform(ks[13], (DEPTH, 2, DN_HEADS), f32, minval=1.0, maxval=16.0)
    dt = jnp.exp(jax.random.uniform(ks[14], (DEPTH, 2, DN_HEADS), f32,
                                    minval=float(np.log(1e-3)), maxval=float(np.log(1e-1))))
    return {
        'x': nrm(ks[0], (BATCH, SEQ, D_MODEL), 1.0),
        'c': nrm(ks[1], (BATCH, D_MODEL), 1.0),
        'ctx': nrm(ks[2], (BATCH, CTX_LEN, D_MODEL), 1.0),
        'c_ctx': nrm(ks[3], (D_MODEL,), 1.0),
        'w_mod': nrm(ks[4], (DEPTH, D_MODEL, 6 * D_MODEL), 0.5 * D_MODEL ** -0.5),
        'b_mod': nrm(ks[5], (DEPTH, 6 * D_MODEL), 0.02),
        'norm1_w': 1.0 + nrm(ks[6], (DEPTH, D_MODEL), 0.02),
        'norm2_w': 1.0 + nrm(ks[7], (DEPTH, D_MODEL), 0.02),
        'w_in': nrm(ks[8], (DEPTH, D_MODEL, IN_COLS), D_MODEL ** -0.5),
        'w_out': nrm(ks[9], (DEPTH, MIX_WIDTH, D_MODEL), MIX_WIDTH ** -0.5),
        'hg_lb': nrm(ks[10], (DEPTH, 2, HG_WIDTH), 0.1),
        'hg_norm_w': 1.0 + nrm(ks[11], (DEPTH, HG_HEAD_DIM), 0.02),
        'dn_conv_w': nrm(ks[12], (DEPTH, CONV_SIZE, CONV_SIZE, 3 * DN_WIDTH), 1.0 / CONV_SIZE),
        'dn_a_log': jnp.log(a_init),
        'dn_dt_bias': dt + jnp.log(-jnp.expm1(-dt)),
        'dn_norm_w': 1.0 + nrm(ks[15], (DEPTH, DN_HEAD_DIM), 0.02),
        'w_group': nrm(ks[16], (DEPTH, D_MODEL, N_GROUPS), D_MODEL ** -0.5),
        'b_group': nrm(ks[17], (DEPTH, N_GROUPS), 0.01),
        'w_expert': nrm(ks[18], (DEPTH, D_MODEL, N_EXPERTS), D_MODEL ** -0.5),
        'b_expert': nrm(ks[19], (DEPTH, N_EXPERTS), 0.01),
        'w_gate': nrm(ks[20], (DEPTH, N_EXPERTS, D_MODEL, EXPERT_FF), D_MODEL ** -0.5),
        'w_up': nrm(ks[21], (DEPTH, N_EXPERTS, D_MODEL, EXPERT_FF), D_MODEL ** -0.5),
        'w_down': nrm(ks[22], (DEPTH, N_EXPERTS, EXPERT_FF, D_MODEL), EXPERT_FF ** -0.5),
        'final_norm_w': 1.0 + nrm(ks[23], (D_MODEL,), 0.02),
    }


def reference(x, c, ctx, c_ctx, w_mod, b_mod, norm1_w, norm2_w, w_in, w_out, hg_lb, hg_norm_w,
              dn_conv_w, dn_a_log, dn_dt_bias, dn_norm_w, w_group, b_group, w_expert, b_expert,
              w_gate, w_up, w_down, final_norm_w):
    lb_all = jnp.cumsum(jax.nn.softmax(hg_lb.astype(jnp.float32), axis=0), axis=0)
    lb_all = (lb_all - lb_all[0]).astype(x.dtype)
    silu_c = jax.nn.silu(c)
    silu_c_ctx = jax.nn.silu(c_ctx)
    xs, cs = x, ctx
    for l in range(DEPTH):
        last = l == DEPTH - 1
        sh1x, sc1x, g1x, sh2x, sc2x, g2x = jnp.split((silu_c @ w_mod[l] + b_mod[l])[:, None, :], 6, axis=-1)
        sh1c, sc1c, g1c, sh2c, sc2c, g2c = jnp.split((silu_c_ctx @ w_mod[l] + b_mod[l])[None, None, :], 6, axis=-1)
        hx = modulate(rms_norm(xs, norm1_w[l]), sh1x, sc1x)
        hc = modulate(rms_norm(cs, norm1_w[l]), sh1c, sc1c)
        oc, ox = hybrid_mixer(hc @ w_in[l], hx @ w_in[l], lb_all[l], hg_norm_w[l], dn_conv_w[l],
                              dn_a_log[l], dn_dt_bias[l], dn_norm_w[l])
        moe_args = (w_group[l], b_group[l], w_expert[l], b_expert[l], w_gate[l], w_up[l], w_down[l])
        xs = xs + g1x * (ox @ w_out[l])
        xs = xs + g2x * hierarchical_moe(modulate(rms_norm(xs, norm2_w[l]), sh2x, sc2x), *moe_args)
        if not last:
            cs = cs + g1c * (oc @ w_out[l])
            cs = cs + g2c * hierarchical_moe(modulate(rms_norm(cs, norm2_w[l]), sh2c, sc2c), *moe_args)
    return rms_norm(xs, final_norm_w)
```

```python
import functools

import jax
import jax.numpy as jnp
from jax import lax
from jax.experimental import pallas as pl
from jax.experimental.pallas import tpu as pltpu

F32 = jnp.float32
BF16 = jnp.bfloat16

D_MODEL = 1024
N_HEADS = 4
HEAD_DIM = 128
GROUP_W = N_HEADS * HEAD_DIM
GRID_W = 64
N_GROUPS = 4
EXPERTS_PER_GROUP = 4
N_EXPERTS = N_GROUPS * EXPERTS_PER_GROUP
EXPERT_FF = 512
NORM_EPS = 1e-6
LANES = 128
IN_COLS = 9 * GROUP_W + 4 * N_HEADS
IN_COLS_PAD = 9 * GROUP_W + LANES
SMALL_BLK = 9 * GROUP_W // LANES
ROWS = 256
TILE = 128
DN_CHUNK = 64
DN_BLOCK = 128
ROUTE_OFF = N_GROUPS
VMEM_LIMIT = 56 * 1024 * 1024

HG_Q, HG_I, HG_G, HG_FF, HG_FB, DN_Q, DN_K, DN_V, DN_Z = range(9)


def _sigmoid(x):
    return 1.0 / (1.0 + jnp.exp(-x))


def _silu(x):
    return x * _sigmoid(x)


def _softplus(x):
    return jnp.maximum(x, 0.0) + jnp.log1p(jnp.exp(-jnp.abs(x)))


def _dot(a, b):
    return jnp.dot(a, b, preferred_element_type=F32)


def _dot_nt(a, b):
    return lax.dot_general(a, b, (((1,), (1,)), ((), ())), preferred_element_type=F32)


def _dot_tn(a, b):
    return lax.dot_general(a, b, (((0,), (0,)), ((), ())), preferred_element_type=F32)


def _rms(x, w):
    return x * lax.rsqrt(jnp.mean(x * x, axis=-1, keepdims=True) + NORM_EPS) * w


def _params(*sem):
    return pltpu.CompilerParams(dimension_semantics=sem, vmem_limit_bytes=VMEM_LIMIT)


def _mod_kernel(c_ref, w_ref, b_ref, o_ref):
    a = _silu(c_ref[...])
    o_ref[0] = jnp.dot(a, w_ref[0], preferred_element_type=F32,
                       precision=lax.Precision.HIGHEST) + b_ref[0]


def _modulation(cvec, w_mod, b_mod):
    depth, d, n = w_mod.shape
    nb = cvec.shape[0]
    tn = 1536
    return pl.pallas_call(
        _mod_kernel,
        out_shape=jax.ShapeDtypeStruct((depth, nb, n), F32),
        grid=(depth, n // tn),
        in_specs=[pl.BlockSpec((nb, d), lambda l, j: (0, 0)),
                  pl.BlockSpec((1, d, tn), lambda l, j: (l, 0, j)),
                  pl.BlockSpec((1, 1, tn), lambda l, j: (l, 0, j))],
        out_specs=pl.BlockSpec((1, nb, tn), lambda l, j: (l, 0, j)),
        compiler_params=_params("arbitrary", "arbitrary"),
        name="adaln_modulation",
    )(cvec, w_mod, b_mod.reshape(depth, 1, n))


def _lb_kernel(depth, lb_ref, o_ref):
    x = [lb_ref[l] for l in range(depth)]
    m = functools.reduce(jnp.maximum, x)
    e = [jnp.exp(v - m) for v in x]
    tot = functools.reduce(lambda a, b: a + b, e)
    sm = [v / tot for v in e]
    cum = sm[0]
    first = cum
    for l in range(depth):
        if l > 0:
            cum = cum + sm[l]
        lb = cum - first
        o_ref[l, 0:2, :] = jnp.log(lb)
        o_ref[l, 2:4, :] = jnp.log1p(-lb)


def _lower_bounds(hg_lb):
    depth = hg_lb.shape[0]
    return pl.pallas_call(
        functools.partial(_lb_kernel, depth),
        out_shape=jax.ShapeDtypeStruct((depth, 4, GROUP_W), F32),
        name="hgrn_lower_bounds",
    )(hg_lb)


def _inproj_kernel(has_y, *refs):
    if has_y:
        x_ref, y_ref, pm_ref, m_ref, nw_ref, w_ref, xo_ref, p_ref = refs
        x = x_ref[0] + pm_ref[0, 5:6, :] * y_ref[0]
        xo_ref[0] = x
    else:
        x_ref, m_ref, nw_ref, w_ref, p_ref = refs
        x = x_ref[0]
    h = _rms(x, nw_ref[...]) * (1.0 + m_ref[0, 1:2, :]) + m_ref[0, 0:1, :]
    p_ref[0] = _dot(h.astype(BF16), w_ref[...])


def _mod_row(nbatch):
    return lambda b, i: (jnp.where(i == 0, nbatch, b), 0, 0)


def _inproj(x, y, prev_mod, mod, norm_w, w_in):
    nbatch, lt, d = x.shape
    npad = w_in.shape[1]
    grid = (nbatch, lt // ROWS)
    xspec = pl.BlockSpec((1, ROWS, d), lambda b, i: (b, i, 0))
    mspec = pl.BlockSpec((1, 6, d), _mod_row(nbatch))
    nwspec = pl.BlockSpec((1, d), lambda b, i: (0, 0))
    wspec = pl.BlockSpec((d, npad), lambda b, i: (0, 0))
    pspec = pl.BlockSpec((1, ROWS, npad), lambda b, i: (b, i, 0))
    pshape = jax.ShapeDtypeStruct((nbatch, lt, npad), F32)
    if y is None:
        return x, pl.pallas_call(
            functools.partial(_inproj_kernel, False),
            out_shape=pshape, grid=grid,
            in_specs=[xspec, mspec, nwspec, wspec], out_specs=pspec,
            compiler_params=_params("arbitrary", "arbitrary"), name="in_proj",
        )(x, mod, norm_w, w_in)
    return pl.pallas_call(
        functools.partial(_inproj_kernel, True),
        out_shape=(jax.ShapeDtypeStruct(x.shape, F32), pshape), grid=grid,
        in_specs=[xspec, xspec, mspec, mspec, nwspec, wspec], out_specs=(xspec, pspec),
        compiler_params=_params("arbitrary", "arbitrary"), name="in_proj_res",
    )(x, y, prev_mod, mod, norm_w, w_in)


def _dnprep_kernel(nblk, prev_ref, x_ref, next_ref, w_ref, o_ref):
    i = pl.program_id(1)
    sec = pl.program_id(2)
    is_ctx = i == 0
    prev_ok = jnp.logical_and(i >= 2, True)
    next_ok = jnp.logical_and(i >= 1, i < nblk - 1)
    zero_halo = jnp.zeros((GRID_W, GROUP_W), F32)
    xe = jnp.concatenate([jnp.where(prev_ok, prev_ref[0], zero_halo), x_ref[0],
                          jnp.where(next_ok, next_ref[0], zero_halo)], axis=0)
    n_ext = ROWS + 2 * GRID_W
    col = lax.broadcasted_iota(jnp.int32, (ROWS, 1), 0) % GRID_W
    acc = jnp.zeros((ROWS, GROUP_W), F32)
    for dw in (-1, 0, 1):
        xs = xe if dw == 0 else pltpu.roll(xe, (-dw) % n_ext, 0)
        if dw == 0:
            keep = None
        else:
            inside = jnp.logical_and(col + dw >= 0, col + dw < GRID_W)
            keep = jnp.logical_or(is_ctx, inside)
        for dh in (-1, 0, 1):
            w = w_ref[(dh + 1) * 3 + (dw + 1):(dh + 1) * 3 + (dw + 2), :]
            if dh != 0:
                w = jnp.where(is_ctx, 0.0, w)
            term = xs[GRID_W * (1 + dh):GRID_W * (1 + dh) + ROWS, :] * w
            acc = acc + (term if keep is None else jnp.where(keep, term, 0.0))
    y = _silu(acc)
    scale = jnp.where(sec == 0, HEAD_DIM ** -0.5, 1.0)
    for h in range(N_HEADS):
        cs = slice(h * HEAD_DIM, (h + 1) * HEAD_DIM)
        yh = y[:, cs]
        nrm = yh * lax.rsqrt(jnp.sum(yh * yh, axis=-1, keepdims=True) + NORM_EPS) * scale
        o_ref[0, :, cs] = jnp.where(sec == 2, yh, nrm)


def _dn_prep(p, conv_w):
    nbatch, lt, _ = p.shape
    nblk = lt // ROWS
    hpb = ROWS // GRID_W
    nhalo = lt // GRID_W
    return pl.pallas_call(
        functools.partial(_dnprep_kernel, nblk),
        out_shape=jax.ShapeDtypeStruct((nbatch, lt, 3 * GROUP_W), F32),
        grid=(nbatch, nblk, 3),
        in_specs=[
            pl.BlockSpec((1, GRID_W, GROUP_W), lambda b, i, s: (b, jnp.maximum(i * hpb - 1, 0), DN_Q + s)),
            pl.BlockSpec((1, ROWS, GROUP_W), lambda b, i, s: (b, i, DN_Q + s)),
            pl.BlockSpec((1, GRID_W, GROUP_W),
                         lambda b, i, s: (b, jnp.minimum((i + 1) * hpb, nhalo - 1), DN_Q + s)),
            pl.BlockSpec((9, GROUP_W), lambda b, i, s: (0, s)),
        ],
        out_specs=pl.BlockSpec((1, ROWS, GROUP_W), lambda b, i, s: (b, i, s)),
        compiler_params=_params("arbitrary", "arbitrary", "arbitrary"), name="dn_prep",
    )(p, p, p, conv_w)


def _block_prefix(g, levels, reverse, axis):
    n = g.shape[axis]
    idx = lax.broadcasted_iota(jnp.int32, g.shape, axis)
    pre, tot = g, g
    out = []
    for j in range(levels):
        out.append((pre, tot))
        half = 1 << j
        upper = (idx & half) != 0
        below = pltpu.roll(tot, half, axis)
        above = pltpu.roll(tot, n - half, axis)
        if reverse:
            pre = pre + jnp.where(upper, 0.0, above)
        else:
            pre = pre + jnp.where(upper, below, 0.0)
        tot = tot + jnp.where(upper, below, above)
    out.append((pre, tot))
    return out


def _scan_blocks(nblk):
    fwd = lambda b, i: i
    bwd = lambda b, i: jnp.where(i == 0, 0, nblk - i)
    return fwd, bwd


def _hgrn_kernel(qf_ref, vf_ref, ff_ref, qb_ref, vb_ref, fb_ref, lb_ref, of_ref, ob_ref, st_ref):
    @pl.when(pl.program_id(1) == 0)
    def _():
        st_ref[...] = jnp.zeros_like(st_ref)

    levels = TILE.bit_length() - 1
    row = lax.broadcasted_iota(jnp.int32, (TILE, TILE), 0)
    colm = lax.broadcasted_iota(jnp.int32, (TILE, TILE), 1)
    eye = row == colm
    lvl_masks = []
    for j in range(levels):
        same = (row >> (j + 1)) == (colm >> (j + 1))
        t_up = (row & (1 << j)) != 0
        s_up = (colm & (1 << j)) != 0
        fwd_m = jnp.logical_and(same, jnp.logical_and(t_up, jnp.logical_not(s_up)))
        bwd_m = jnp.logical_and(same, jnp.logical_and(jnp.logical_not(t_up), s_up))
        lvl_masks.append((fwd_m, bwd_m))
    ridx = lax.broadcasted_iota(jnp.int32, (ROWS, GROUP_W), 0)

    for d, (q_ref, v_ref, f_ref, o_ref) in enumerate(((qf_ref, vf_ref, ff_ref, of_ref),
                                                      (qb_ref, vb_ref, fb_ref, ob_ref))):
        reverse = d == 1
        q = _silu(q_ref[0])
        v = v_ref[0]
        x = f_ref[0]
        log_lb = lb_ref[d:d + 1, :]
        log_1mlb = lb_ref[2 + d:3 + d, :]
        b_term = log_1mlb - _softplus(-x)
        mx = jnp.maximum(log_lb, b_term)
        g = mx + jnp.log1p(jnp.exp(-jnp.abs(log_lb - b_term)))
        k = 1.0 - jnp.exp(g)
        lv = _block_prefix(g, levels, reverse, 0)
        pre, tot = lv[levels]
        q_in = (q * jnp.exp(pre)).astype(BF16)
        k_out = (k * jnp.exp(tot - pre)).astype(BF16)
        g_tile = jnp.exp(tot)
        qe, ke = [], []
        for j in range(levels):
            pj, tj = lv[j]
            upper = (ridx & (1 << j)) != 0
            q_side = jnp.logical_not(upper) if reverse else upper
            e = jnp.exp(jnp.where(q_side, pj, tj - pj))
            qe.append((q * e).astype(BF16))
            ke.append((k * e).astype(BF16))
        qb16 = q.astype(BF16)
        kb16 = k.astype(BF16)
        vb16 = v.astype(BF16)
        ntile = ROWS // TILE
        for tix in (range(ntile - 1, -1, -1) if reverse else range(ntile)):
            rs = slice(tix * TILE, (tix + 1) * TILE)
            for h in range(N_HEADS):
                cs = slice(h * HEAD_DIM, (h + 1) * HEAD_DIM)
                attn = jnp.where(eye, _dot_nt(qb16[rs, cs], kb16[rs, cs]), 0.0)
                for j in range(levels):
                    a = _dot_nt(qe[j][rs, cs], ke[j][rs, cs])
                    attn = attn + jnp.where(lvl_masks[j][d], a, 0.0)
                st = st_ref[d * N_HEADS + h]
                o = _dot(attn.astype(BF16), vb16[rs, cs]) + _dot_nt(q_in[rs, cs], st.astype(BF16))
                o_ref[0, rs, cs] = o
                upd = _dot_tn(vb16[rs, cs], k_out[rs, cs])
                st_ref[d * N_HEADS + h] = st * g_tile[tix * TILE:tix * TILE + 1, cs] + upd


def _hgrn(p, lb):
    nbatch, lt, _ = p.shape
    nblk = lt // ROWS
    fwd, bwd = _scan_blocks(nblk)

    def spec(blk, cb):
        return pl.BlockSpec((1, ROWS, GROUP_W), lambda b, i: (b, blk(b, i), cb))

    oshape = jax.ShapeDtypeStruct((nbatch, lt, GROUP_W), F32)
    return pl.pallas_call(
        _hgrn_kernel,
        out_shape=(oshape, oshape),
        grid=(nbatch, nblk),
        in_specs=[spec(fwd, HG_Q), spec(fwd, HG_I), spec(fwd, HG_FF),
                  spec(bwd, HG_Q), spec(bwd, HG_I), spec(bwd, HG_FB),
                  pl.BlockSpec((4, GROUP_W), lambda b, i: (0, 0))],
        out_specs=(spec(fwd, 0), spec(bwd, 0)),
        scratch_shapes=[pltpu.VMEM((2 * N_HEADS, HEAD_DIM, HEAD_DIM), F32)],
        compiler_params=_params("arbitrary", "arbitrary"), name="hgrn2_scan",
    )(p, p, p, p, p, p, lb)


def _gdn_kernel(qf_ref, kf_ref, vf_ref, sf_ref, tf_ref, qb_ref, kb_ref, vb_ref, sb_ref, tb_ref,
                ac_ref, ar_ref, of_ref, ob_ref, s_ref):
    @pl.when(pl.program_id(1) == 0)
    def _():
        s_ref[...] = jnp.zeros_like(s_ref)

    c = DN_CHUNK
    tb = DN_BLOCK
    levels = c.bit_length() - 1
    row = lax.broadcasted_iota(jnp.int32, (tb, tb), 0)
    colm = lax.broadcasted_iota(jnp.int32, (tb, tb), 1)
    same = (row >> levels) == (colm >> levels)
    eye = (row == colm).astype(F32)
    nchunk = ROWS // c

    chains, solves = [], []
    for d, refs in enumerate(((qf_ref, kf_ref, vf_ref, sf_ref, tf_ref), (qb_ref, kb_ref, vb_ref, sb_ref, tb_ref))):
        q_ref, k_ref, v_ref, sm_ref, smt_ref = refs
        reverse = d == 1
        sm = sm_ref[0]
        smt = smt_ref[0]
        beta_c = _sigmoid(sm)
        la_c = -jnp.exp(ac_ref[0:1, :]) * _softplus(sm + ac_ref[1:2, :])
        la_r = -jnp.exp(ar_ref[:, 0:1]) * _softplus(smt + ar_ref[:, 1:2])
        cum_c, tot_c = _block_prefix(la_c, levels, reverse, 0)[levels]
        cum_r, _ = _block_prefix(la_r, levels, reverse, 1)[levels]
        incl = jnp.logical_and(same, (colm >= row) if reverse else (colm <= row))
        strict = jnp.logical_and(same, (colm > row) if reverse else (colm < row))
        q_all, k_all, v_all = q_ref[0], k_ref[0], v_ref[0]
        for h in range(N_HEADS):
            cs = slice(h * HEAD_DIM, (h + 1) * HEAD_DIM)
            bl = 4 * d + h
            al = 8 + 4 * d + h
            cc = cum_c[:, al:al + 1]
            tt = tot_c[:, al:al + 1]
            ch = dict(d=d, h=h, reverse=reverse, w=[], u=[], qk=[],
                      q_dec=(q_all[:, cs] * jnp.exp(cc)).astype(BF16),
                      k_dec=(k_all[:, cs] * jnp.exp(tt - cc)).astype(BF16),
                      g_tot=jnp.exp(tt), s=s_ref[d * N_HEADS + h], out=[None] * nchunk)
            chains.append(ch)
            for sb in range(ROWS // tb):
                rs = slice(sb * tb, (sb + 1) * tb)
                solves.append(dict(ch=ch, q=q_all[rs, cs], k=k_all[rs, cs], v=v_all[rs, cs], cc=cc[rs],
                                   cr=cum_r[al:al + 1, rs], bc=beta_c[rs, bl:bl + 1], incl=incl, strict=strict))
    for sv in solves:
        sv["decay"] = jnp.where(sv["incl"], jnp.exp(jnp.where(sv["incl"], sv["cc"] - sv["cr"], 0.0)), 0.0)
        sv["kb"] = sv["k"].astype(BF16)
    for sv in solves:
        sv["kk"] = _dot_nt(sv["kb"], sv["kb"])
    for sv in solves:
        sv["qk"] = _dot_nt(sv["q"].astype(BF16), sv["kb"])
    for sv in solves:
        sv["ch"]["qk"].append((sv["qk"] * sv["decay"]).astype(BF16))
        sv["pw"] = jnp.where(sv["strict"], sv["bc"] * sv["kk"] * sv["decay"], 0.0)
        sv["tinv"] = eye - sv["pw"]
    for _ in range(levels - 1):
        for sv in solves:
            pwb = sv["pw"].astype(BF16)
            sv["pw"] = _dot(pwb, pwb)
        for sv in solves:
            sv["tinv"] = sv["tinv"] + _dot(sv["tinv"].astype(BF16), sv["pw"].astype(BF16))
    for sv in solves:
        rhs = jnp.concatenate([sv["k"] * (sv["bc"] * jnp.exp(sv["cc"])), sv["v"] * sv["bc"]], axis=1)
        sv["wu"] = _dot(sv["tinv"].astype(BF16), rhs.astype(BF16))
    for sv in solves:
        sv["ch"]["w"].append(sv["wu"][:, :HEAD_DIM])
        sv["ch"]["u"].append(sv["wu"][:, HEAD_DIM:])

    per = tb // c
    for step in range(nchunk):
        for ch in chains:
            ci = nchunk - 1 - step if ch["reverse"] else step
            ch["ci"], ch["rs"] = ci, slice(ci * c, (ci + 1) * c)
            ch["sbi"] = ci // per
            ch["sub"] = slice((ci % per) * c, (ci % per) * c + c)
            wq = jnp.concatenate([ch["w"][ch["sbi"]][ch["sub"]].astype(BF16), ch["q_dec"][ch["rs"]]], axis=0)
            ch["ws"] = _dot(wq, ch["s"].astype(BF16))
        for ch in chains:
            ch["v_new"] = (ch["u"][ch["sbi"]][ch["sub"]] - ch["ws"][:c]).astype(BF16)
            ch["upd"] = _dot_tn(ch["k_dec"][ch["rs"]], ch["v_new"])
        for ch in chains:
            ci = ch["ci"]
            ch["out"][ci] = ch["ws"][c:] + _dot(ch["qk"][ch["sbi"]][ch["sub"], ch["sub"]], ch["v_new"])
            ch["s"] = ch["g_tot"][ci * c:ci * c + 1] * ch["s"] + ch["upd"]

    for d, o_ref in enumerate((of_ref, ob_ref)):
        mine = [ch for ch in chains if ch["d"] == d]
        o_ref[0] = jnp.concatenate([jnp.concatenate(ch["out"], axis=0) for ch in mine], axis=1)
        for ch in mine:
            s_ref[d * N_HEADS + ch["h"]] = ch["s"]


def _gdn(qkv, p, small_t, acoef_c, acoef_r):
    nbatch, lt, _ = qkv.shape
    nblk = lt // ROWS
    fwd, bwd = _scan_blocks(nblk)

    def spec(blk, cb):
        return pl.BlockSpec((1, ROWS, GROUP_W), lambda b, i: (b, blk(b, i), cb))

    def small(blk):
        return pl.BlockSpec((1, ROWS, LANES), lambda b, i: (b, blk(b, i), SMALL_BLK))

    def small_tr(blk):
        return pl.BlockSpec((1, 16, ROWS), lambda b, i: (b, 0, blk(b, i)))

    oshape = jax.ShapeDtypeStruct((nbatch, lt, GROUP_W), F32)
    return pl.pallas_call(
        _gdn_kernel,
        out_shape=(oshape, oshape),
        grid=(nbatch, nblk),
        in_specs=[spec(fwd, 0), spec(fwd, 1), spec(fwd, 2), small(fwd), small_tr(fwd),
                  spec(bwd, 0), spec(bwd, 1), spec(bwd, 2), small(bwd), small_tr(bwd),
                  pl.BlockSpec((8, LANES), lambda b, i: (0, 0)),
                  pl.BlockSpec((16, LANES), lambda b, i: (0, 0))],
        out_specs=(spec(fwd, 0), spec(bwd, 0)),
        scratch_shapes=[pltpu.VMEM((2 * N_HEADS, HEAD_DIM, HEAD_DIM), F32)],
        compiler_params=_params("arbitrary", "arbitrary"), name="gated_deltanet_scan",
    )(qkv, qkv, qkv, p, small_t, qkv, qkv, qkv, p, small_t, acoef_c, acoef_r)


def _outproj_kernel(x_ref, hf_ref, hb_ref, df_ref, db_ref, g_ref, z_ref, m_ref, hw_ref, dw_ref,
                    wo_ref, n2_ref, wr_ref, br_ref, xo_ref, h2_ref, lg_ref):
    parts = []
    for (a_ref, b_ref, gate_ref, nw_ref) in ((hf_ref, hb_ref, g_ref, hw_ref), (df_ref, db_ref, z_ref, dw_ref)):
        o = a_ref[0] + b_ref[0]
        gate = gate_ref[0]
        for h in range(N_HEADS):
            cs = slice(h * HEAD_DIM, (h + 1) * HEAD_DIM)
            parts.append((_rms(o[:, cs], nw_ref[...]) * _silu(gate[:, cs])).astype(BF16))
    y = jnp.concatenate(parts, axis=1)
    x1 = x_ref[0] + m_ref[0, 2:3, :] * _dot(y, wo_ref[...])
    xo_ref[0] = x1
    h2 = _rms(x1, n2_ref[...]) * (1.0 + m_ref[0, 4:5, :]) + m_ref[0, 3:4, :]
    h_hi = h2.astype(BF16)
    h2_ref[0] = h_hi
    h_lo = (h2 - h_hi.astype(F32)).astype(BF16)
    wr = wr_ref[...]
    w_hi = wr.astype(BF16)
    w_lo = (wr - w_hi.astype(F32)).astype(BF16)
    lg_ref[0] = _dot(h_hi, w_hi) + (_dot(h_hi, w_lo) + _dot(h_lo, w_hi)) + br_ref[...]


def _outproj(x, ohf, ohb, odf, odb, p, mod, hg_nw, dn_nw, w_out, norm2_w, w_route, b_route):
    nbatch, lt, d = x.shape
    grid = (nbatch, lt // ROWS)
    xspec = pl.BlockSpec((1, ROWS, d), lambda b, i: (b, i, 0))
    ospec = pl.BlockSpec((1, ROWS, GROUP_W), lambda b, i: (b, i, 0))

    def pcol(cb):
        return pl.BlockSpec((1, ROWS, GROUP_W), lambda b, i: (b, i, cb))

    def full(shape):
        return pl.BlockSpec(shape, lambda b, i: tuple(0 for _ in shape))

    return pl.pallas_call(
        _outproj_kernel,
        out_shape=(jax.ShapeDtypeStruct(x.shape, F32), jax.ShapeDtypeStruct(x.shape, BF16),
                   jax.ShapeDtypeStruct((nbatch, lt, LANES), F32)),
        grid=grid,
        in_specs=[xspec, ospec, ospec, ospec, ospec, pcol(HG_G), pcol(DN_Z),
                  pl.BlockSpec((1, 6, d), _mod_row(nbatch)),
                  full((1, HEAD_DIM)), full((1, HEAD_DIM)), full((d, d)), full((1, d)),
                  full((d, LANES)), full((1, LANES))],
        out_specs=(xspec, xspec, pl.BlockSpec((1, ROWS, LANES), lambda b, i: (b, i, 0))),
        compiler_params=_params("arbitrary", "arbitrary"), name="out_proj",
    )(x, ohf, ohb, odf, odb, p, p, mod, hg_nw, dn_nw, w_out, norm2_w, w_route, b_route)


def _router_kernel(lg_ref, o_ref):
    lg = lg_ref[...]
    lane = lax.broadcasted_iota(jnp.int32, lg.shape, 1).astype(F32)
    neg = -jnp.inf
    big = float(LANES)

    def first_argmax(vals):
        m = jnp.max(vals, axis=-1, keepdims=True)
        return m, jnp.min(jnp.where(vals == m, lane, big), axis=-1, keepdims=True)

    in_groups = lane < N_GROUPS
    gl = jnp.where(in_groups, lg, neg)
    gmax, gidx = first_argmax(gl)
    g_w = 1.0 / jnp.sum(jnp.where(in_groups, jnp.exp(gl - gmax), 0.0), axis=-1, keepdims=True)
    lo = ROUTE_OFF + EXPERTS_PER_GROUP * gidx
    el = jnp.where(jnp.logical_and(lane >= lo, lane < lo + EXPERTS_PER_GROUP), lg, neg)
    m1, i1 = first_argmax(el)
    el2 = jnp.where(lane == i1, neg, el)
    m2, i2 = first_argmax(el2)
    e2 = jnp.exp(m2 - m1)
    w1 = g_w / (1.0 + e2)
    w2 = g_w * e2 / (1.0 + e2)
    o_ref[...] = jnp.where(lane == i1, w1, 0.0) + jnp.where(lane == i2, w2, 0.0)


def _token_tile(t):
    return max(m for m in range(ROWS, 1024 + 1, ROWS) if t % m == 0)


def _router(logits):
    t = logits.shape[0]
    tm = _token_tile(t)
    return pl.pallas_call(
        _router_kernel,
        out_shape=jax.ShapeDtypeStruct(logits.shape, F32),
        grid=(t // tm,),
        in_specs=[pl.BlockSpec((tm, LANES), lambda i: (i, 0))],
        out_specs=pl.BlockSpec((tm, LANES), lambda i: (i, 0)),
        compiler_params=_params("arbitrary"), name="router",
    )(logits)


def _moe_kernel(h_ref, gate_ref, wg_ref, wu_ref, wd_ref, o_ref):
    e = pl.program_id(1)

    @pl.when(e == 0)
    def _():
        o_ref[...] = jnp.zeros_like(o_ref)

    h = h_ref[...]
    a = _silu(_dot(h, wg_ref[0])) * _dot(h, wu_ref[0])
    gates = gate_ref[...]
    lane = lax.broadcasted_iota(jnp.int32, gates.shape, 1)
    gate = jnp.sum(jnp.where(lane == ROUTE_OFF + e, gates, 0.0), axis=-1, keepdims=True)
    o_ref[...] += gate * _dot(a.astype(BF16), wd_ref[0])


def _moe(h2, gates, w_gate, w_up, w_down):
    t, d = h2.shape
    ne, _, ff = w_gate.shape
    tm = _token_tile(t)
    return pl.pallas_call(
        _moe_kernel,
        out_shape=jax.ShapeDtypeStruct((t, d), F32),
        grid=(t // tm, ne),
        in_specs=[pl.BlockSpec((tm, d), lambda i, e: (i, 0)),
                  pl.BlockSpec((tm, LANES), lambda i, e: (i, 0)),
                  pl.BlockSpec((1, d, ff), lambda i, e: (e, 0, 0)),
                  pl.BlockSpec((1, d, ff), lambda i, e: (e, 0, 0)),
                  pl.BlockSpec((1, ff, d), lambda i, e: (e, 0, 0))],
        out_specs=pl.BlockSpec((tm, d), lambda i, e: (i, 0)),
        compiler_params=_params("arbitrary", "arbitrary"), name="moe_experts",
    )(h2, gates, w_gate, w_up, w_down)


def _final_kernel(x_ref, y_ref, m_ref, nw_ref, o_ref):
    x = x_ref[0] + m_ref[0, 5:6, :] * y_ref[0]
    o_ref[0] = _rms(x, nw_ref[...])


def _final(x, y, mod, norm_w, ctx_blocks):
    nbatch, lt, d = x.shape
    nlat = lt // ROWS - ctx_blocks
    xspec = pl.BlockSpec((1, ROWS, d), lambda b, i: (b, i + ctx_blocks, 0))
    return pl.pallas_call(
        _final_kernel,
        out_shape=jax.ShapeDtypeStruct((nbatch, nlat * ROWS, d), F32),
        grid=(nbatch, nlat),
        in_specs=[xspec, xspec, pl.BlockSpec((1, 6, d), lambda b, i: (b, 0, 0)),
                  pl.BlockSpec((1, d), lambda b, i: (0, 0))],
        out_specs=pl.BlockSpec((1, ROWS, d), lambda b, i: (b, i, 0)),
        compiler_params=_params("arbitrary", "arbitrary"), name="final_norm",
    )(x, y, mod, norm_w)


def kernel(x, c, ctx, c_ctx, w_mod, b_mod, norm1_w, norm2_w, w_in, w_out, hg_lb, hg_norm_w, dn_conv_w,
           dn_a_log, dn_dt_bias, dn_norm_w, w_group, b_group, w_expert, b_expert, w_gate, w_up, w_down,
           final_norm_w):
    nbatch, seq, d = x.shape
    ctx_len = ctx.shape[1]
    depth = w_mod.shape[0]
    assert d == D_MODEL and ctx_len == ROWS and seq % ROWS == 0 and ROWS % GRID_W == 0
    assert w_in.shape[2] == IN_COLS
    lt = ctx_len + seq

    nb = -(-(nbatch + 1) // 8) * 8
    cvec = jnp.concatenate([c, c_ctx[None, :], jnp.zeros((nb - nbatch - 1, d), F32)], axis=0)
    mods = _modulation(cvec, w_mod, b_mod).reshape(depth, nb, 6, d)
    lbs = _lower_bounds(hg_lb)

    w_in_p = jnp.pad(w_in, ((0, 0), (0, 0), (0, IN_COLS_PAD - IN_COLS))).astype(BF16)
    w_out_b = w_out.astype(BF16)
    w_gate_b, w_up_b, w_down_b = w_gate.astype(BF16), w_up.astype(BF16), w_down.astype(BF16)
    nroute = N_GROUPS + N_EXPERTS
    w_route = jnp.pad(jnp.concatenate([w_group, w_expert], axis=2), ((0, 0), (0, 0), (0, LANES - nroute)))
    b_route = jnp.pad(jnp.concatenate([b_group, b_expert], axis=1), ((0, 0), (0, LANES - nroute)))
    conv_w = dn_conv_w.reshape(depth, 9, 3 * GROUP_W)
    a_flat = dn_a_log.reshape(depth, 2 * N_HEADS)
    dt_flat = dn_dt_bias.reshape(depth, 2 * N_HEADS)
    lane_vals = lambda v: jnp.pad(v, ((0, 0), (2 * N_HEADS, LANES - 4 * N_HEADS)))
    acoef_c = jnp.pad(jnp.stack([lane_vals(a_flat), lane_vals(dt_flat)], axis=1), ((0, 0), (0, 6), (0, 0)))
    row_vals = lambda v: jnp.pad(v, ((0, 0), (2 * N_HEADS, 0)))
    acoef_r = jnp.pad(jnp.stack([row_vals(a_flat), row_vals(dt_flat)], axis=2), ((0, 0), (0, 0), (0, LANES - 2)))

    xs = jnp.concatenate([ctx, x], axis=1)
    y = None
    for l in range(depth):
        xs, p = _inproj(xs, y, mods[l - 1] if l else None, mods[l], norm1_w[l][None, :], w_in_p[l])
        small_t = jnp.swapaxes(p[:, :, 9 * GROUP_W:9 * GROUP_W + 4 * N_HEADS], 1, 2)
        qkv = _dn_prep(p, conv_w[l])
        ohf, ohb = _hgrn(p, lbs[l])
        odf, odb = _gdn(qkv, p, small_t, acoef_c[l], acoef_r[l])
        xs, h2, logits = _outproj(xs, ohf, ohb, odf, odb, p, mods[l], hg_norm_w[l][None, :],
                                  dn_norm_w[l][None, :], w_out_b[l], norm2_w[l][None, :],
                                  w_route[l], b_route[l][None, :])
        gates = _router(logits.reshape(nbatch * lt, LANES))
        y = _moe(h2.reshape(nbatch * lt, d), gates, w_gate_b[l], w_up_b[l], w_down_b[l]).reshape(nbatch, lt, d)
    return _final(xs, y, mods[depth - 1], final_norm_w[None, :], ctx_len // ROWS)
```

```python
import functools

import jax
import jax.numpy as jnp
from jax import lax
from jax.experimental import pallas as pl
from jax.experimental.pallas import tpu as pltpu

F32 = jnp.float32
BF16 = jnp.bfloat16

D_MODEL = 1024
N_HEADS = 4
HEAD_DIM = 128
GROUP_W = N_HEADS * HEAD_DIM
GRID_W = 64
N_GROUPS = 4
EXPERTS_PER_GROUP = 4
N_EXPERTS = N_GROUPS * EXPERTS_PER_GROUP
EXPERT_FF = 512
NORM_EPS = 1e-6
LANES = 128
IN_COLS = 9 * GROUP_W + 4 * N_HEADS
IN_COLS_PAD = 9 * GROUP_W + LANES
SMALL_BLK = 9 * GROUP_W // LANES
ROWS = 256
TILE = 128
DN_CHUNK = 64
DN_BLOCK = 128
ROUTE_OFF = N_GROUPS
VMEM_LIMIT = 56 * 1024 * 1024

HG_Q, HG_I, HG_G, HG_FF, HG_FB, DN_Q, DN_K, DN_V, DN_Z = range(9)


def _sigmoid(x):
    return 1.0 / (1.0 + jnp.exp(-x))


def _silu(x):
    return x * _sigmoid(x)


def _softplus(x):
    return jnp.maximum(x, 0.0) + jnp.log1p(jnp.exp(-jnp.abs(x)))


def _dot(a, b):
    return jnp.dot(a, b, preferred_element_type=F32)


def _dot_nt(a, b):
    return lax.dot_general(a, b, (((1,), (1,)), ((), ())), preferred_element_type=F32)


def _dot_tn(a, b):
    return lax.dot_general(a, b, (((0,), (0,)), ((), ())), preferred_element_type=F32)


def _rms(x, w):
    return x * lax.rsqrt(jnp.mean(x * x, axis=-1, keepdims=True) + NORM_EPS) * w


def _params(*sem):
    return pltpu.CompilerParams(dimension_semantics=sem, vmem_limit_bytes=VMEM_LIMIT)


def _mod_kernel(c_ref, w_ref, b_ref, o_ref):
    a = _silu(c_ref[...])
    o_ref[0] = jnp.dot(a, w_ref[0], preferred_element_type=F32,
                       precision=lax.Precision.HIGHEST) + b_ref[0]


def _modulation(cvec, w_mod, b_mod):
    depth, d, n = w_mod.shape
    nb = cvec.shape[0]
    tn = 1536
    return pl.pallas_call(
        _mod_kernel,
        out_shape=jax.ShapeDtypeStruct((depth, nb, n), F32),
        grid=(depth, n // tn),
        in_specs=[pl.BlockSpec((nb, d), lambda l, j: (0, 0)),
                  pl.BlockSpec((1, d, tn), lambda l, j: (l, 0, j)),
                  pl.BlockSpec((1, 1, tn), lambda l, j: (l, 0, j))],
        out_specs=pl.BlockSpec((1, nb, tn), lambda l, j: (l, 0, j)),
        compiler_params=_params("arbitrary", "arbitrary"),
        name="adaln_modulation",
    )(cvec, w_mod, b_mod.reshape(depth, 1, n))


def _lb_kernel(depth, lb_ref, o_ref):
    x = [lb_ref[l] for l in range(depth)]
    m = functools.reduce(jnp.maximum, x)
    e = [jnp.exp(v - m) for v in x]
    tot = functools.reduce(lambda a, b: a + b, e)
    sm = [v / tot for v in e]
    cum = sm[0]
    first = cum
    for l in range(depth):
        if l > 0:
            cum = cum + sm[l]
        lb = cum - first
        o_ref[l, 0:2, :] = jnp.log(lb)
        o_ref[l, 2:4, :] = jnp.log1p(-lb)


def _lower_bounds(hg_lb):
    depth = hg_lb.shape[0]
    return pl.pallas_call(
        functools.partial(_lb_kernel, depth),
        out_shape=jax.ShapeDtypeStruct((depth, 4, GROUP_W), F32),
        name="hgrn_lower_bounds",
    )(hg_lb)


def _inproj_kernel(has_y, *refs):
    if has_y:
        x_ref, y_ref, pm_ref, m_ref, nw_ref, w_ref, xo_ref, p_ref = refs
        x = x_ref[0] + pm_ref[0, 5:6, :] * y_ref[0]
        xo_ref[0] = x
    else:
        x_ref, m_ref, nw_ref, w_ref, p_ref = refs
        x = x_ref[0]
    h = _rms(x, nw_ref[...]) * (1.0 + m_ref[0, 1:2, :]) + m_ref[0, 0:1, :]
    p_ref[0] = _dot(h.astype(BF16), w_ref[...])


def _mod_row(nbatch):
    return lambda b, i: (jnp.where(i == 0, nbatch, b), 0, 0)


def _inproj(x, y, prev_mod, mod, norm_w, w_in):
    nbatch, lt, d = x.shape
    npad = w_in.shape[1]
    grid = (nbatch, lt // ROWS)
    xspec = pl.BlockSpec((1, ROWS, d), lambda b, i: (b, i, 0))
    mspec = pl.BlockSpec((1, 6, d), _mod_row(nbatch))
    nwspec = pl.BlockSpec((1, d), lambda b, i: (0, 0))
    wspec = pl.BlockSpec((d, npad), lambda b, i: (0, 0))
    pspec = pl.BlockSpec((1, ROWS, npad), lambda b, i: (b, i, 0))
    pshape = jax.ShapeDtypeStruct((nbatch, lt, npad), F32)
    if y is None:
        return x, pl.pallas_call(
            functools.partial(_inproj_kernel, False),
            out_shape=pshape, grid=grid,
            in_specs=[xspec, mspec, nwspec, wspec], out_specs=pspec,
            compiler_params=_params("arbitrary", "arbitrary"), name="in_proj",
        )(x, mod, norm_w, w_in)
    return pl.pallas_call(
        functools.partial(_inproj_kernel, True),
        out_shape=(jax.ShapeDtypeStruct(x.shape, F32), pshape), grid=grid,
        in_specs=[xspec, xspec, mspec, mspec, nwspec, wspec], out_specs=(xspec, pspec),
        compiler_params=_params("arbitrary", "arbitrary"), name="in_proj_res",
    )(x, y, prev_mod, mod, norm_w, w_in)


def _dnprep_kernel(nblk, prev_ref, x_ref, next_ref, w_ref, o_ref):
    i = pl.program_id(1)
    sec = pl.program_id(2)
    is_ctx = i == 0
    prev_ok = jnp.logical_and(i >= 2, True)
    next_ok = jnp.logical_and(i >= 1, i < nblk - 1)
    zero_halo = jnp.zeros((GRID_W, GROUP_W), F32)
    xe = jnp.concatenate([jnp.where(prev_ok, prev_ref[0], zero_halo), x_ref[0],
                          jnp.where(next_ok, next_ref[0], zero_halo)], axis=0)
    n_ext = ROWS + 2 * GRID_W
    col = lax.broadcasted_iota(jnp.int32, (ROWS, 1), 0) % GRID_W
    acc = jnp.zeros((ROWS, GROUP_W), F32)
    for dw in (-1, 0, 1):
        xs = xe if dw == 0 else pltpu.roll(xe, (-dw) % n_ext, 0)
        if dw == 0:
            keep = None
        else:
            inside = jnp.logical_and(col + dw >= 0, col + dw < GRID_W)
            keep = jnp.logical_or(is_ctx, inside)
        for dh in (-1, 0, 1):
            w = w_ref[(dh + 1) * 3 + (dw + 1):(dh + 1) * 3 + (dw + 2), :]
            if dh != 0:
                w = jnp.where(is_ctx, 0.0, w)
            term = xs[GRID_W * (1 + dh):GRID_W * (1 + dh) + ROWS, :] * w
            acc = acc + (term if keep is None else jnp.where(keep, term, 0.0))
    y = _silu(acc)
    scale = jnp.where(sec == 0, HEAD_DIM ** -0.5, 1.0)
    for h in range(N_HEADS):
        cs = slice(h * HEAD_DIM, (h + 1) * HEAD_DIM)
        yh = y[:, cs]
        nrm = yh * lax.rsqrt(jnp.sum(yh * yh, axis=-1, keepdims=True) + NORM_EPS) * scale
        o_ref[0, :, cs] = jnp.where(sec == 2, yh, nrm)


def _dn_prep(p, conv_w):
    nbatch, lt, _ = p.shape
    nblk = lt // ROWS
    hpb = ROWS // GRID_W
    nhalo = lt // GRID_W
    return pl.pallas_call(
        functools.partial(_dnprep_kernel, nblk),
        out_shape=jax.ShapeDtypeStruct((nbatch, lt, 3 * GROUP_W), F32),
        grid=(nbatch, nblk, 3),
        in_specs=[
            pl.BlockSpec((1, GRID_W, GROUP_W), lambda b, i, s: (b, jnp.maximum(i * hpb - 1, 0), DN_Q + s)),
            pl.BlockSpec((1, ROWS, GROUP_W), lambda b, i, s: (b, i, DN_Q + s)),
            pl.BlockSpec((1, GRID_W, GROUP_W),
                         lambda b, i, s: (b, jnp.minimum((i + 1) * hpb, nhalo - 1), DN_Q + s)),
            pl.BlockSpec((9, GROUP_W), lambda b, i, s: (0, s)),
        ],
        out_specs=pl.BlockSpec((1, ROWS, GROUP_W), lambda b, i, s: (b, i, s)),
        compiler_params=_params("arbitrary", "arbitrary", "arbitrary"), name="dn_prep",
    )(p, p, p, conv_w)


def _block_prefix(g, levels, reverse, axis):
    n = g.shape[axis]
    idx = lax.broadcasted_iota(jnp.int32, g.shape, axis)
    pre, tot = g, g
    out = []
    for j in range(levels):
        out.append((pre, tot))
        half = 1 << j
        upper = (idx & half) != 0
        below = pltpu.roll(tot, half, axis)
        above = pltpu.roll(tot, n - half, axis)
        if reverse:
            pre = pre + jnp.where(upper, 0.0, above)
        else:
            pre = pre + jnp.where(upper, below, 0.0)
        tot = tot + jnp.where(upper, below, above)
    out.append((pre, tot))
    return out


def _scan_blocks(nblk):
    fwd = lambda b, i: i
    bwd = lambda b, i: jnp.where(i == 0, 0, nblk - i)
    return fwd, bwd


def _hgrn_kernel(qf_ref, vf_ref, ff_ref, qb_ref, vb_ref, fb_ref, lb_ref, of_ref, ob_ref, st_ref):
    @pl.when(pl.program_id(1) == 0)
    def _():
        st_ref[...] = jnp.zeros_like(st_ref)

    levels = TILE.bit_length() - 1
    row = lax.broadcasted_iota(jnp.int32, (TILE, TILE), 0)
    colm = lax.broadcasted_iota(jnp.int32, (TILE, TILE), 1)
    eye = row == colm
    lvl_masks = []
    for j in range(levels):
        same = (row >> (j + 1)) == (colm >> (j + 1))
        t_up = (row & (1 << j)) != 0
        s_up = (colm & (1 << j)) != 0
        fwd_m = jnp.logical_and(same, jnp.logical_and(t_up, jnp.logical_not(s_up)))
        bwd_m = jnp.logical_and(same, jnp.logical_and(jnp.logical_not(t_up), s_up))
        lvl_masks.append((fwd_m, bwd_m))
    ridx = lax.broadcasted_iota(jnp.int32, (ROWS, GROUP_W), 0)

    for d, (q_ref, v_ref, f_ref, o_ref) in enumerate(((qf_ref, vf_ref, ff_ref, of_ref),
                                                      (qb_ref, vb_ref, fb_ref, ob_ref))):
        reverse = d == 1
        q = _silu(q_ref[0])
        v = v_ref[0]
        x = f_ref[0]
        log_lb = lb_ref[d:d + 1, :]
        log_1mlb = lb_ref[2 + d:3 + d, :]
        b_term = log_1mlb - _softplus(-x)
        mx = jnp.maximum(log_lb, b_term)
        g = mx + jnp.log1p(jnp.exp(-jnp.abs(log_lb - b_term)))
        k = 1.0 - jnp.exp(g)
        lv = _block_prefix(g, levels, reverse, 0)
        pre, tot = lv[levels]
        q_in = (q * jnp.exp(pre)).astype(BF16)
        k_out = (k * jnp.exp(tot - pre)).astype(BF16)
        g_tile = jnp.exp(tot)
        qe, ke = [], []
        for j in range(levels):
            pj, tj = lv[j]
            upper = (ridx & (1 << j)) != 0
            q_side = jnp.logical_not(upper) if reverse else upper
            e = jnp.exp(jnp.where(q_side, pj, tj - pj))
            qe.append((q * e).astype(BF16))
            ke.append((k * e).astype(BF16))
        qb16 = q.astype(BF16)
        kb16 = k.astype(BF16)
        vb16 = v.astype(BF16)
        ntile = ROWS // TILE
        for tix in (range(ntile - 1, -1, -1) if reverse else range(ntile)):
            rs = slice(tix * TILE, (tix + 1) * TILE)
            for h in range(N_HEADS):
                cs = slice(h * HEAD_DIM, (h + 1) * HEAD_DIM)
                attn = jnp.where(eye, _dot_nt(qb16[rs, cs], kb16[rs, cs]), 0.0)
                for j in range(levels):
                    a = _dot_nt(qe[j][rs, cs], ke[j][rs, cs])
                    attn = attn + jnp.where(lvl_masks[j][d], a, 0.0)
                st = st_ref[d * N_HEADS + h]
                o = _dot(attn.astype(BF16), vb16[rs, cs]) + _dot_nt(q_in[rs, cs], st.astype(BF16))
                o_ref[0, rs, cs] = o
                upd = _dot_tn(vb16[rs, cs], k_out[rs, cs])
                st_ref[d * N_HEADS + h] = st * g_tile[tix * TILE:tix * TILE + 1, cs] + upd


def _hgrn(p, lb):
    nbatch, lt, _ = p.shape
    nblk = lt // ROWS
    fwd, bwd = _scan_blocks(nblk)

    def spec(blk, cb):
        return pl.BlockSpec((1, ROWS, GROUP_W), lambda b, i: (b, blk(b, i), cb))

    oshape = jax.ShapeDtypeStruct((nbatch, lt, GROUP_W), F32)
    return pl.pallas_call(
        _hgrn_kernel,
        out_shape=(oshape, oshape),
        grid=(nbatch, nblk),
        in_specs=[spec(fwd, HG_Q), spec(fwd, HG_I), spec(fwd, HG_FF),
                  spec(bwd, HG_Q), spec(bwd, HG_I), spec(bwd, HG_FB),
                  pl.BlockSpec((4, GROUP_W), lambda b, i: (0, 0))],
        out_specs=(spec(fwd, 0), spec(bwd, 0)),
        scratch_shapes=[pltpu.VMEM((2 * N_HEADS, HEAD_DIM, HEAD_DIM), F32)],
        compiler_params=_params("arbitrary", "arbitrary"), name="hgrn2_scan",
    )(p, p, p, p, p, p, lb)


def _gdn_kernel(qf_ref, kf_ref, vf_ref, sf_ref, tf_ref, qb_ref, kb_ref, vb_ref, sb_ref, tb_ref,
                ac_ref, ar_ref, of_ref, ob_ref, s_ref):
    @pl.when(pl.program_id(1) == 0)
    def _():
        s_ref[...] = jnp.zeros_like(s_ref)

    c = DN_CHUNK
    tb = DN_BLOCK
    levels = c.bit_length() - 1
    row = lax.broadcasted_iota(jnp.int32, (tb, tb), 0)
    colm = lax.broadcasted_iota(jnp.int32, (tb, tb), 1)
    same = (row >> levels) == (colm >> levels)
    eye = (row == colm).astype(F32)
    nchunk = ROWS // c

    chains, solves = [], []
    for d, refs in enumerate(((qf_ref, kf_ref, vf_ref, sf_ref, tf_ref), (qb_ref, kb_ref, vb_ref, sb_ref, tb_ref))):
        q_ref, k_ref, v_ref, sm_ref, smt_ref = refs
        reverse = d == 1
        sm = sm_ref[0]
        smt = smt_ref[0]
        beta_c = _sigmoid(sm)
        la_c = -jnp.exp(ac_ref[0:1, :]) * _softplus(sm + ac_ref[1:2, :])
        la_r = -jnp.exp(ar_ref[:, 0:1]) * _softplus(smt + ar_ref[:, 1:2])
        cum_c, tot_c = _block_prefix(la_c, levels, reverse, 0)[levels]
        cum_r, _ = _block_prefix(la_r, levels, reverse, 1)[levels]
        incl = jnp.logical_and(same, (colm >= row) if reverse else (colm <= row))
        strict = jnp.logical_and(same, (colm > row) if reverse else (colm < row))
        q_all, k_all, v_all = q_ref[0], k_ref[0], v_ref[0]
        for h in range(N_HEADS):
            cs = slice(h * HEAD_DIM, (h + 1) * HEAD_DIM)
            bl = 4 * d + h
            al = 8 + 4 * d + h
            cc = cum_c[:, al:al + 1]
            tt = tot_c[:, al:al + 1]
            ch = dict(d=d, h=h, reverse=reverse, w=[], u=[], qk=[],
                      q_dec=(q_all[:, cs] * jnp.exp(cc)).astype(BF16),
                      k_dec=(k_all[:, cs] * jnp.exp(tt - cc)).astype(BF16),
                      g_tot=jnp.exp(tt), s=s_ref[d * N_HEADS + h], out=[None] * nchunk)
            chains.append(ch)
            for sb in range(ROWS // tb):
                rs = slice(sb * tb, (sb + 1) * tb)
                solves.append(dict(ch=ch, q=q_all[rs, cs], k=k_all[rs, cs], v=v_all[rs, cs], cc=cc[rs],
                                   cr=cum_r[al:al + 1, rs], bc=beta_c[rs, bl:bl + 1], incl=incl, strict=strict))
    for sv in solves:
        sv["decay"] = jnp.where(sv["incl"], jnp.exp(jnp.where(sv["incl"], sv["cc"] - sv["cr"], 0.0)), 0.0)
        sv["kb"] = sv["k"].astype(BF16)
    for sv in solves:
        sv["kk"] = _dot_nt(sv["kb"], sv["kb"])
    for sv in solves:
        sv["qk"] = _dot_nt(sv["q"].astype(BF16), sv["kb"])
    for sv in solves:
        sv["ch"]["qk"].append((sv["qk"] * sv["decay"]).astype(BF16))
        sv["pw"] = jnp.where(sv["strict"], sv["bc"] * sv["kk"] * sv["decay"], 0.0)
        sv["tinv"] = eye - sv["pw"]
    for _ in range(levels - 1):
        for sv in solves:
            pwb = sv["pw"].astype(BF16)
            sv["pw"] = _dot(pwb, pwb)
        for sv in solves:
            sv["tinv"] = sv["tinv"] + _dot(sv["tinv"].astype(BF16), sv["pw"].astype(BF16))
    for sv in solves:
        rhs = jnp.concatenate([sv["k"] * (sv["bc"] * jnp.exp(sv["cc"])), sv["v"] * sv["bc"]], axis=1)
        sv["wu"] = _dot(sv["tinv"].astype(BF16), rhs.astype(BF16))
    for sv in solves:
        sv["ch"]["w"].append(sv["wu"][:, :HEAD_DIM])
        sv["ch"]["u"].append(sv["wu"][:, HEAD_DIM:])

    per = tb // c
    for step in range(nchunk):
        for ch in chains:
            ci = nchunk - 1 - step if ch["reverse"] else step
            ch["ci"], ch["rs"] = ci, slice(ci * c, (ci + 1) * c)
            ch["sbi"] = ci // per
            ch["sub"] = slice((ci % per) * c, (ci % per) * c + c)
            wq = jnp.concatenate([ch["w"][ch["sbi"]][ch["sub"]].astype(BF16), ch["q_dec"][ch["rs"]]], axis=0)
            ch["ws"] = _dot(wq, ch["s"].astype(BF16))
        for ch in chains:
            ch["v_new"] = (ch["u"][ch["sbi"]][ch["sub"]] - ch["ws"][:c]).astype(BF16)
            ch["upd"] = _dot_tn(ch["k_dec"][ch["rs"]], ch["v_new"])
        for ch in chains:
            ci = ch["ci"]
            ch["out"][ci] = ch["ws"][c:] + _dot(ch["qk"][ch["sbi"]][ch["sub"], ch["sub"]], ch["v_new"])
            ch["s"] = ch["g_tot"][ci * c:ci * c + 1] * ch["s"] + ch["upd"]

    for d, o_ref in enumerate((of_ref, ob_ref)):
        mine = [ch for ch in chains if ch["d"] == d]
        o_ref[0] = jnp.concatenate([jnp.concatenate(ch["out"], axis=0) for ch in mine], axis=1)
        for ch in mine:
            s_ref[d * N_HEADS + ch["h"]] = ch["s"]


def _gdn(qkv, p, small_t, acoef_c, acoef_r):
    nbatch, lt, _ = qkv.shape
    nblk = lt // ROWS
    fwd, bwd = _scan_blocks(nblk)

    def spec(blk, cb):
        return pl.BlockSpec((1, ROWS, GROUP_W), lambda b, i: (b, blk(b, i), cb))

    def small(blk):
        return pl.BlockSpec((1, ROWS, LANES), lambda b, i: (b, blk(b, i), SMALL_BLK))

    def small_tr(blk):
        return pl.BlockSpec((1, 16, ROWS), lambda b, i: (b, 0, blk(b, i)))

    oshape = jax.ShapeDtypeStruct((nbatch, lt, GROUP_W), F32)
    return pl.pallas_call(
        _gdn_kernel,
        out_shape=(oshape, oshape),
        grid=(nbatch, nblk),
        in_specs=[spec(fwd, 0), spec(fwd, 1), spec(fwd, 2), small(fwd), small_tr(fwd),
                  spec(bwd, 0), spec(bwd, 1), spec(bwd, 2), small(bwd), small_tr(bwd),
                  pl.BlockSpec((8, LANES), lambda b, i: (0, 0)),
                  pl.BlockSpec((16, LANES), lambda b, i: (0, 0))],
        out_specs=(spec(fwd, 0), spec(bwd, 0)),
        scratch_shapes=[pltpu.VMEM((2 * N_HEADS, HEAD_DIM, HEAD_DIM), F32)],
        compiler_params=_params("arbitrary", "arbitrary"), name="gated_deltanet_scan",
    )(qkv, qkv, qkv, p, small_t, qkv, qkv, qkv, p, small_t, acoef_c, acoef_r)


def _outproj_kernel(x_ref, hf_ref, hb_ref, df_ref, db_ref, g_ref, z_ref, m_ref, hw_ref, dw_ref,
                    wo_ref, n2_ref, wr_ref, br_ref, xo_ref, h2_ref, lg_ref):
    parts = []
    for (a_ref, b_ref, gate_ref, nw_ref) in ((hf_ref, hb_ref, g_ref, hw_ref), (df_ref, db_ref, z_ref, dw_ref)):
        o = a_ref[0] + b_ref[0]
        gate = gate_ref[0]
        for h in range(N_HEADS):
            cs = slice(h * HEAD_DIM, (h + 1) * HEAD_DIM)
            parts.append((_rms(o[:, cs], nw_ref[...]) * _silu(gate[:, cs])).astype(BF16))
    y = jnp.concatenate(parts, axis=1)
    x1 = x_ref[0] + m_ref[0, 2:3, :] * _dot(y, wo_ref[...])
    xo_ref[0] = x1
    h2 = _rms(x1, n2_ref[...]) * (1.0 + m_ref[0, 4:5, :]) + m_ref[0, 3:4, :]
    h2_ref[0] = h2
    h_hi = h2.astype(BF16)
    h_lo = (h2 - h_hi.astype(F32)).astype(BF16)
    wr = wr_ref[...]
    w_hi = wr.astype(BF16)
    w_lo = (wr - w_hi.astype(F32)).astype(BF16)
    lg_ref[0] = _dot(h_hi, w_hi) + (_dot(h_hi, w_lo) + _dot(h_lo, w_hi)) + br_ref[...]


def _outproj(x, ohf, ohb, odf, odb, p, mod, hg_nw, dn_nw, w_out, norm2_w, w_route, b_route):
    nbatch, lt, d = x.shape
    grid = (nbatch, lt // ROWS)
    xspec = pl.BlockSpec((1, ROWS, d), lambda b, i: (b, i, 0))
    ospec = pl.BlockSpec((1, ROWS, GROUP_W), lambda b, i: (b, i, 0))

    def pcol(cb):
        return pl.BlockSpec((1, ROWS, GROUP_W), lambda b, i: (b, i, cb))

    def full(shape):
        return pl.BlockSpec(shape, lambda b, i: tuple(0 for _ in shape))

    return pl.pallas_call(
        _outproj_kernel,
        out_shape=(jax.ShapeDtypeStruct(x.shape, F32), jax.ShapeDtypeStruct(x.shape, F32),
                   jax.ShapeDtypeStruct((nbatch, lt, LANES), F32)),
        grid=grid,
        in_specs=[xspec, ospec, ospec, ospec, ospec, pcol(HG_G), pcol(DN_Z),
                  pl.BlockSpec((1, 6, d), _mod_row(nbatch)),
                  full((1, HEAD_DIM)), full((1, HEAD_DIM)), full((d, d)), full((1, d)),
                  full((d, LANES)), full((1, LANES))],
        out_specs=(xspec, xspec, pl.BlockSpec((1, ROWS, LANES), lambda b, i: (b, i, 0))),
        compiler_params=_params("arbitrary", "arbitrary"), name="out_proj",
    )(x, ohf, ohb, odf, odb, p, p, mod, hg_nw, dn_nw, w_out, norm2_w, w_route, b_route)


PAIRS = [(a, b) for a in range(EXPERTS_PER_GROUP) for b in range(a + 1, EXPERTS_PER_GROUP)]
N_CLASSES = N_GROUPS * len(PAIRS)

def _router_kernel(lg_ref, o_ref):
    lg = lg_ref[...]
    lane = lax.broadcasted_iota(jnp.int32, lg.shape, 1).astype(F32)
    neg = -jnp.inf
    big = float(LANES)

    def first_argmax(vals):
        m = jnp.max(vals, axis=-1, keepdims=True)
        return m, jnp.min(jnp.where(vals == m, lane, big), axis=-1, keepdims=True)

    in_groups = lane < N_GROUPS
    gl = jnp.where(in_groups, lg, neg)
    gmax, gidx = first_argmax(gl)
    g_w = 1.0 / jnp.sum(jnp.where(in_groups, jnp.exp(gl - gmax), 0.0), axis=-1, keepdims=True)
    lo = ROUTE_OFF + EXPERTS_PER_GROUP * gidx
    el = jnp.where(jnp.logical_and(lane >= lo, lane < lo + EXPERTS_PER_GROUP), lg, neg)
    m1, i1 = first_argmax(el)
    el2 = jnp.where(lane == i1, neg, el)
    m2, i2 = first_argmax(el2)
    e2 = jnp.exp(m2 - m1)
    w1 = g_w / (1.0 + e2)
    w2 = g_w * e2 / (1.0 + e2)
    first_lower = i1 < i2
    a = jnp.minimum(i1, i2) - lo
    b = jnp.maximum(i1, i2) - lo
    pair = a * (EXPERTS_PER_GROUP - 1) - a * (a - 1.0) * 0.5 + (b - a - 1.0)
    cls = gidx * float(len(PAIRS)) + pair
    o_ref[...] = (jnp.where(lane == 0.0, cls, 0.0)
                  + jnp.where(lane == 1.0, jnp.where(first_lower, w1, w2), 0.0)
                  + jnp.where(lane == 2.0, jnp.where(first_lower, w2, w1), 0.0))


def _token_tile(t):
    return max(m for m in range(ROWS, 1024 + 1, ROWS) if t % m == 0)


def _router(logits):
    t = logits.shape[0]
    tm = _token_tile(t)
    return pl.pallas_call(
        _router_kernel,
        out_shape=jax.ShapeDtypeStruct(logits.shape, F32),
        grid=(t // tm,),
        in_specs=[pl.BlockSpec((tm, LANES), lambda i: (i, 0))],
        out_specs=pl.BlockSpec((tm, LANES), lambda i: (i, 0)),
        compiler_params=_params("arbitrary"), name="router",
    )(logits)


EXPERT_TILE = 256
GATHER_TILE = 512


def _route_plan(rinfo):
    t = rinfo.shape[0]
    cls = rinfo[:, 0].astype(jnp.int32)
    onehot = (cls[:, None] == jnp.arange(N_CLASSES, dtype=jnp.int32)[None, :]).astype(jnp.int32)
    counts = jnp.sum(onehot, axis=0)
    tiles_c = (counts + EXPERT_TILE - 1) // EXPERT_TILE
    tile_end = jnp.cumsum(tiles_c)
    tile_start = tile_end - tiles_c
    rank = jnp.take_along_axis(jnp.cumsum(onehot, axis=0) - onehot, cls[:, None], axis=1)[:, 0]
    pos = tile_start[cls] * EXPERT_TILE + rank
    ntiles = t // EXPERT_TILE + N_CLASSES
    src = jnp.zeros((ntiles * EXPERT_TILE,), jnp.int32).at[pos].set(jnp.arange(t, dtype=jnp.int32))
    gates = jnp.zeros((ntiles * EXPERT_TILE, 2), F32).at[pos].set(rinfo[:, 1:3])
    tile_id = jnp.arange(ntiles, dtype=jnp.int32)
    tile_cls = jnp.minimum(jnp.searchsorted(tile_end, tile_id, side="right").astype(jnp.int32), N_CLASSES - 1)
    used = (tile_id < tile_end[-1]).astype(jnp.int32)
    pair_a = jnp.array([a for a, _ in PAIRS], jnp.int32)
    pair_b = jnp.array([b for _, b in PAIRS], jnp.int32)
    group = tile_cls // len(PAIRS)
    ea = group * EXPERTS_PER_GROUP + pair_a[tile_cls % len(PAIRS)]
    eb = group * EXPERTS_PER_GROUP + pair_b[tile_cls % len(PAIRS)]
    return pos, src, jnp.pad(gates, ((0, 0), (0, LANES - 2))), ea, eb, used


def _gather_kernel(idx_ref, src_ref, o_ref, sem):
    rows = o_ref.shape[0]

    def issue(r, carry):
        pltpu.make_async_copy(src_ref.at[pl.ds(idx_ref[0, 0, r], 1), :], o_ref.at[pl.ds(r, 1), :], sem).start()
        return carry

    lax.fori_loop(0, rows, issue, 0, unroll=8)
    pltpu.make_async_copy(src_ref.at[pl.ds(0, rows), :], o_ref, sem).wait()


def _gather_rows(src, idx):
    n = idx.shape[0]
    d = src.shape[1]
    tm = GATHER_TILE
    return pl.pallas_call(
        _gather_kernel,
        out_shape=jax.ShapeDtypeStruct((n, d), src.dtype),
        grid=(n // tm,),
        in_specs=[pl.BlockSpec((1, 1, tm), lambda i: (i, 0, 0), memory_space=pltpu.SMEM),
                  pl.BlockSpec(memory_space=pl.ANY)],
        out_specs=pl.BlockSpec((tm, d), lambda i: (i, 0)),
        scratch_shapes=[pltpu.SemaphoreType.DMA],
        compiler_params=_params("arbitrary"), name="gather_rows",
    )(idx.reshape(n // tm, 1, tm), src)


def _expert_kernel(ea_ref, eb_ref, used_ref, x_ref, gate_ref, wga_ref, wua_ref, wda_ref,
                   wgb_ref, wub_ref, wdb_ref, o_ref):
    i = pl.program_id(0)

    @pl.when(used_ref[i] != 0)
    def _():
        x = x_ref[...].astype(BF16)
        gates = gate_ref[...]
        act_a = _silu(_dot(x, wga_ref[0])) * _dot(x, wua_ref[0]) * gates[:, 0:1]
        act_b = _silu(_dot(x, wgb_ref[0])) * _dot(x, wub_ref[0]) * gates[:, 1:2]
        o_ref[...] = _dot(act_a.astype(BF16), wda_ref[0]) + _dot(act_b.astype(BF16), wdb_ref[0])

    @pl.when(used_ref[i] == 0)
    def _():
        o_ref[...] = jnp.zeros_like(o_ref)


def _experts(xs_sorted, gates, ea, eb, used, w_gate, w_up, w_down):
    n, d = xs_sorted.shape
    _, _, ff = w_gate.shape
    tm = EXPERT_TILE
    wa = lambda i, ea, eb, used: (ea[i], 0, 0)
    wb = lambda i, ea, eb, used: (eb[i], 0, 0)
    row = lambda i, ea, eb, used: (i, 0)
    return pl.pallas_call(
        _expert_kernel,
        out_shape=jax.ShapeDtypeStruct((n, d), F32),
        grid_spec=pltpu.PrefetchScalarGridSpec(
            num_scalar_prefetch=3,
            grid=(n // tm,),
            in_specs=[pl.BlockSpec((tm, d), row), pl.BlockSpec((tm, LANES), row),
                      pl.BlockSpec((1, d, ff), wa), pl.BlockSpec((1, d, ff), wa), pl.BlockSpec((1, ff, d), wa),
                      pl.BlockSpec((1, d, ff), wb), pl.BlockSpec((1, d, ff), wb), pl.BlockSpec((1, ff, d), wb)],
            out_specs=pl.BlockSpec((tm, d), row)),
        compiler_params=_params("arbitrary"), name="moe_experts",
    )(ea, eb, used, xs_sorted, gates, w_gate, w_up, w_down, w_gate, w_up, w_down)


def _moe(h2, rinfo, w_gate, w_up, w_down):
    pos, src, gates, ea, eb, used = _route_plan(rinfo)
    xs_sorted = _gather_rows(h2, src)
    ys_sorted = _experts(xs_sorted, gates, ea, eb, used, w_gate, w_up, w_down)
    return _gather_rows(ys_sorted, pos)


def _final_kernel(x_ref, y_ref, m_ref, nw_ref, o_ref):
    x = x_ref[0] + m_ref[0, 5:6, :] * y_ref[0]
    o_ref[0] = _rms(x, nw_ref[...])


def _final(x, y, mod, norm_w, ctx_blocks):
    nbatch, lt, d = x.shape
    nlat = lt // ROWS - ctx_blocks
    xspec = pl.BlockSpec((1, ROWS, d), lambda b, i: (b, i + ctx_blocks, 0))
    return pl.pallas_call(
        _final_kernel,
        out_shape=jax.ShapeDtypeStruct((nbatch, nlat * ROWS, d), F32),
        grid=(nbatch, nlat),
        in_specs=[xspec, xspec, pl.BlockSpec((1, 6, d), lambda b, i: (b, 0, 0)),
                  pl.BlockSpec((1, d), lambda b, i: (0, 0))],
        out_specs=pl.BlockSpec((1, ROWS, d), lambda b, i: (b, i, 0)),
        compiler_params=_params("arbitrary", "arbitrary"), name="final_norm",
    )(x, y, mod, norm_w)


def kernel(x, c, ctx, c_ctx, w_mod, b_mod, norm1_w, norm2_w, w_in, w_out, hg_lb, hg_norm_w, dn_conv_w,
           dn_a_log, dn_dt_bias, dn_norm_w, w_group, b_group, w_expert, b_expert, w_gate, w_up, w_down,
           final_norm_w):
    nbatch, seq, d = x.shape
    ctx_len = ctx.shape[1]
    depth = w_mod.shape[0]
    assert d == D_MODEL and ctx_len == ROWS and seq % ROWS == 0 and ROWS % GRID_W == 0
    assert w_in.shape[2] == IN_COLS
    lt = ctx_len + seq

    nb = -(-(nbatch + 1) // 8) * 8
    cvec = jnp.concatenate([c, c_ctx[None, :], jnp.zeros((nb - nbatch - 1, d), F32)], axis=0)
    mods = _modulation(cvec, w_mod, b_mod).reshape(depth, nb, 6, d)
    lbs = _lower_bounds(hg_lb)

    w_in_p = jnp.pad(w_in, ((0, 0), (0, 0), (0, IN_COLS_PAD - IN_COLS))).astype(BF16)
    w_out_b = w_out.astype(BF16)
    w_gate_b, w_up_b, w_down_b = w_gate.astype(BF16), w_up.astype(BF16), w_down.astype(BF16)
    nroute = N_GROUPS + N_EXPERTS
    w_route = jnp.pad(jnp.concatenate([w_group, w_expert], axis=2), ((0, 0), (0, 0), (0, LANES - nroute)))
    b_route = jnp.pad(jnp.concatenate([b_group, b_expert], axis=1), ((0, 0), (0, LANES - nroute)))
    conv_w = dn_conv_w.reshape(depth, 9, 3 * GROUP_W)
    a_flat = dn_a_log.reshape(depth, 2 * N_HEADS)
    dt_flat = dn_dt_bias.reshape(depth, 2 * N_HEADS)
    lane_vals = lambda v: jnp.pad(v, ((0, 0), (2 * N_HEADS, LANES - 4 * N_HEADS)))
    acoef_c = jnp.pad(jnp.stack([lane_vals(a_flat), lane_vals(dt_flat)], axis=1), ((0, 0), (0, 6), (0, 0)))
    row_vals = lambda v: jnp.pad(v, ((0, 0), (2 * N_HEADS, 0)))
    acoef_r = jnp.pad(jnp.stack([row_vals(a_flat), row_vals(dt_flat)], axis=2), ((0, 0), (0, 0), (0, LANES - 2)))

    xs = jnp.concatenate([ctx, x], axis=1)
    y = None
    for l in range(depth):
        xs, p = _inproj(xs, y, mods[l - 1] if l else None, mods[l], norm1_w[l][None, :], w_in_p[l])
        small_t = jnp.swapaxes(p[:, :, 9 * GROUP_W:9 * GROUP_W + 4 * N_HEADS], 1, 2)
        qkv = _dn_prep(p, conv_w[l])
        ohf, ohb = _hgrn(p, lbs[l])
        odf, odb = _gdn(qkv, p, small_t, acoef_c[l], acoef_r[l])
        xs, h2, logits = _outproj(xs, ohf, ohb, odf, odb, p, mods[l], hg_norm_w[l][None, :],
                                  dn_norm_w[l][None, :], w_out_b[l], norm2_w[l][None, :],
                                  w_route[l], b_route[l][None, :])
        rinfo = _router(logits.reshape(nbatch * lt, LANES))
        y = _moe(h2.reshape(nbatch * lt, d), rinfo, w_gate_b[l], w_up_b[l], w_down_b[l]).reshape(nbatch, lt, d)
    return _final(xs, y, mods[depth - 1], final_norm_w[None, :], ctx_len // ROWS)
```

```python
import functools

import jax
import jax.numpy as jnp
from jax import lax
from jax.experimental import pallas as pl
from jax.experimental.pallas import tpu as pltpu

F32 = jnp.float32
BF16 = jnp.bfloat16

D_MODEL = 1024
N_HEADS = 4
HEAD_DIM = 128
GROUP_W = N_HEADS * HEAD_DIM
GRID_W = 64
N_GROUPS = 4
EXPERTS_PER_GROUP = 4
N_EXPERTS = N_GROUPS * EXPERTS_PER_GROUP
EXPERT_FF = 512
NORM_EPS = 1e-6
LANES = 128
IN_COLS = 9 * GROUP_W + 4 * N_HEADS
REST_COLS = 6 * GROUP_W + LANES
SMALL_BLK = 6 * GROUP_W // LANES
ROWS = 256
TILE = 128
DN_CHUNK = 64
DN_BLOCK = 128
ROUTE_OFF = N_GROUPS
PAIRS = [(a, b) for a in range(EXPERTS_PER_GROUP) for b in range(a + 1, EXPERTS_PER_GROUP)]
N_CLASSES = N_GROUPS * len(PAIRS)
ROUTE_CLS, ROUTE_GATE_A, ROUTE_GATE_B, ROUTE_RANK = 0, 1, 2, 3
VMEM_LIMIT = 56 * 1024 * 1024

HG_Q, HG_I, HG_G, HG_FF, HG_FB, DN_Z = range(6)


def _sigmoid(x):
    return 1.0 / (1.0 + jnp.exp(-x))


def _silu(x):
    return x * _sigmoid(x)


def _softplus(x):
    return jnp.maximum(x, 0.0) + jnp.log(1.0 + jnp.exp(-jnp.abs(x)))


def _dot(a, b):
    return jnp.dot(a, b, preferred_element_type=F32)


def _dot_nt(a, b):
    return lax.dot_general(a, b, (((1,), (1,)), ((), ())), preferred_element_type=F32)


def _dot_tn(a, b):
    return lax.dot_general(a, b, (((0,), (0,)), ((), ())), preferred_element_type=F32)


def _rms(x, w):
    return x * lax.rsqrt(jnp.mean(x * x, axis=-1, keepdims=True) + NORM_EPS) * w


def _params(*sem):
    return pltpu.CompilerParams(dimension_semantics=sem, vmem_limit_bytes=VMEM_LIMIT)


def _mod_kernel(c_ref, w_ref, b_ref, o_ref):
    a = _silu(c_ref[...])
    o_ref[0] = jnp.dot(a, w_ref[0], preferred_element_type=F32,
                       precision=lax.Precision.HIGHEST) + b_ref[0]


def _modulation(cvec, w_mod, b_mod):
    depth, d, n = w_mod.shape
    nb = cvec.shape[0]
    tn = 1536
    return pl.pallas_call(
        _mod_kernel,
        out_shape=jax.ShapeDtypeStruct((depth, nb, n), F32),
        grid=(depth, n // tn),
        in_specs=[pl.BlockSpec((nb, d), lambda l, j: (0, 0)),
                  pl.BlockSpec((1, d, tn), lambda l, j: (l, 0, j)),
                  pl.BlockSpec((1, 1, tn), lambda l, j: (l, 0, j))],
        out_specs=pl.BlockSpec((1, nb, tn), lambda l, j: (l, 0, j)),
        compiler_params=_params("arbitrary", "arbitrary"),
        name="adaln_modulation",
    )(cvec, w_mod, b_mod.reshape(depth, 1, n))


def _lb_kernel(depth, lb_ref, o_ref):
    x = [lb_ref[l] for l in range(depth)]
    m = functools.reduce(jnp.maximum, x)
    e = [jnp.exp(v - m) for v in x]
    tot = functools.reduce(lambda a, b: a + b, e)
    sm = [v / tot for v in e]
    cum = sm[0]
    first = cum
    for l in range(depth):
        if l > 0:
            cum = cum + sm[l]
        lb = cum - first
        o_ref[l, 0:2, :] = jnp.log(lb)
        o_ref[l, 2:4, :] = jnp.log1p(-lb)
        o_ref[l, 4:6, :] = 1.0 - lb
        o_ref[l, 6:8, :] = jnp.zeros_like(lb)


def _lower_bounds(hg_lb):
    depth = hg_lb.shape[0]
    return pl.pallas_call(
        functools.partial(_lb_kernel, depth),
        out_shape=jax.ShapeDtypeStruct((depth, 8, GROUP_W), F32),
        name="hgrn_lower_bounds",
    )(hg_lb)


def _short_conv(xe, w_ref, cs, is_ctx):
    n_ext = xe.shape[0]
    col = lax.broadcasted_iota(jnp.int32, (n_ext, 1), 0) % GRID_W
    acc = None
    for dw in (-1, 0, 1):
        if dw == 0:
            xs = xe
        else:
            xs = pltpu.roll(xe, (-dw) % n_ext, 0)
            inside = jnp.logical_and(col + dw >= 0, col + dw < GRID_W)
            xs = jnp.where(jnp.logical_or(is_ctx, inside), xs, 0.0)
        for dh in (-1, 0, 1):
            tap = (dh + 1) * 3 + (dw + 1)
            w = w_ref[tap:tap + 1, cs]
            if dh != 0:
                w = jnp.where(is_ctx, 0.0, w)
            term = xs[GRID_W * (1 + dh):GRID_W * (1 + dh) + ROWS, :] * w
            acc = term if acc is None else acc + term
    return acc


def _inproj_kernel(has_y, nblk, *refs):
    if has_y:
        (x_ref, xp_ref, xn_ref, y_ref, yp_ref, yn_ref, pm_ref, m_ref, nw_ref, wq_ref, wr_ref, cw_ref,
         xo_ref, qkv_ref, p_ref) = refs
        g2 = pm_ref[0, 5:6, :]
        x = x_ref[0] + g2 * y_ref[0]
        xo_ref[0] = x
        x_prev = xp_ref[0] + g2 * yp_ref[0]
        x_next = xn_ref[0] + g2 * yn_ref[0]
    else:
        x_ref, xp_ref, xn_ref, m_ref, nw_ref, wq_ref, wr_ref, cw_ref, qkv_ref, p_ref = refs
        x, x_prev, x_next = x_ref[0], xp_ref[0], xn_ref[0]
    i = pl.program_id(1)
    is_ctx = i == 0
    prev_ok = i >= 2
    next_ok = jnp.logical_and(i >= 1, i < nblk - 1)
    xe = jnp.concatenate([x_prev, x, x_next], axis=0)
    he = (_rms(xe, nw_ref[...]) * (1.0 + m_ref[0, 1:2, :]) + m_ref[0, 0:1, :]).astype(BF16)
    pq = _dot(he, wq_ref[...])
    row = lax.broadcasted_iota(jnp.int32, (ROWS + 2 * GRID_W, 1), 0)
    halo_ok = jnp.logical_and(jnp.logical_or(row >= GRID_W, prev_ok),
                              jnp.logical_or(row < GRID_W + ROWS, next_ok))
    pq = jnp.where(halo_ok, pq, 0.0)
    p_ref[0] = _dot(he[GRID_W:GRID_W + ROWS], wr_ref[...])
    for sec in range(3):
        for h in range(N_HEADS):
            cs = slice(sec * GROUP_W + h * HEAD_DIM, sec * GROUP_W + (h + 1) * HEAD_DIM)
            yh = _silu(_short_conv(pq[:, cs], cw_ref, cs, is_ctx))
            if sec < 2:
                yh = yh * lax.rsqrt(jnp.sum(yh * yh, axis=-1, keepdims=True) + NORM_EPS)
                if sec == 0:
                    yh = yh * HEAD_DIM ** -0.5
            qkv_ref[0, :, cs] = yh


def _mod_row(nbatch):
    return lambda b, i: (jnp.where(i == 0, nbatch, b), 0, 0)


def _inproj(x, y, prev_mod, mod, norm_w, w_qkv, w_rest, conv_w):
    nbatch, lt, d = x.shape
    nblk = lt // ROWS
    hpb = ROWS // GRID_W
    nhalo = lt // GRID_W
    nrest = w_rest.shape[1]
    nqkv = w_qkv.shape[1]
    grid = (nbatch, nblk)
    xspec = pl.BlockSpec((1, ROWS, d), lambda b, i: (b, i, 0))
    prev = pl.BlockSpec((1, GRID_W, d), lambda b, i: (b, jnp.maximum(i * hpb - 1, 0), 0))
    nxt = pl.BlockSpec((1, GRID_W, d), lambda b, i: (b, jnp.minimum((i + 1) * hpb, nhalo - 1), 0))
    mspec = pl.BlockSpec((1, 6, d), _mod_row(nbatch))

    def full(shape):
        return pl.BlockSpec(shape, lambda b, i: tuple(0 for _ in shape))

    wspecs = [full((1, d)), full((d, nqkv)), full((d, nrest)), full((9, nqkv))]
    outs = (jax.ShapeDtypeStruct((nbatch, lt, nqkv), F32), jax.ShapeDtypeStruct((nbatch, lt, nrest), F32))
    ospecs = (pl.BlockSpec((1, ROWS, nqkv), lambda b, i: (b, i, 0)),
              pl.BlockSpec((1, ROWS, nrest), lambda b, i: (b, i, 0)))
    if y is None:
        qkv, p = pl.pallas_call(
            functools.partial(_inproj_kernel, False, nblk),
            out_shape=outs, grid=grid,
            in_specs=[xspec, prev, nxt, mspec] + wspecs, out_specs=ospecs,
            compiler_params=_params("arbitrary", "arbitrary"), name="in_proj",
        )(x, x, x, mod, norm_w, w_qkv, w_rest, conv_w)
        return x, qkv, p
    return pl.pallas_call(
        functools.partial(_inproj_kernel, True, nblk),
        out_shape=(jax.ShapeDtypeStruct(x.shape, F32),) + outs, grid=grid,
        in_specs=[xspec, prev, nxt, xspec, prev, nxt, mspec, mspec] + wspecs, out_specs=(xspec,) + ospecs,
        compiler_params=_params("arbitrary", "arbitrary"), name="in_proj_res",
    )(x, x, x, y, y, y, prev_mod, mod, norm_w, w_qkv, w_rest, conv_w)


def _block_prefix(g, levels, reverse, axis):
    n = g.shape[axis]
    idx = lax.broadcasted_iota(jnp.int32, g.shape, axis)
    pre, tot = g, g
    out = []
    for j in range(levels):
        out.append((pre, tot))
        half = 1 << j
        upper = (idx & half) != 0
        below = pltpu.roll(tot, half, axis)
        above = pltpu.roll(tot, n - half, axis)
        if reverse:
            pre = pre + jnp.where(upper, 0.0, above)
        else:
            pre = pre + jnp.where(upper, below, 0.0)
        tot = tot + jnp.where(upper, below, above)
    out.append((pre, tot))
    return out


def _pair_level_exponents(g, levels, reverse):
    rows, width = g.shape
    sub = 8
    idx = lax.broadcasted_iota(jnp.int32, g.shape, 0)
    pre, tot = g, g
    args = []
    for j in range(levels):
        half = 1 << j
        if half < sub:
            upper = (idx & half) != 0
            q_side = jnp.logical_not(upper) if reverse else upper
            args.append(jnp.where(q_side, pre, tot - pre))
            below = pltpu.roll(tot, half, 0)
            above = pltpu.roll(tot, rows - half, 0)
            if reverse:
                pre = pre + jnp.where(upper, 0.0, above)
            else:
                pre = pre + jnp.where(upper, below, 0.0)
            tot = tot + jnp.where(upper, below, above)
        else:
            nb = rows // (2 * half)
            p4 = pre.reshape(nb, 2, half, width)
            t4 = tot.reshape(nb, 2, half, width)
            p_lo, p_hi, t_lo, t_hi = p4[:, 0], p4[:, 1], t4[:, 0], t4[:, 1]
            both = t_lo + t_hi
            if reverse:
                arg = jnp.stack([p_lo, t_hi - p_hi], axis=1)
                pre = jnp.stack([p_lo + t_hi, p_hi], axis=1)
            else:
                arg = jnp.stack([t_lo - p_lo, p_hi], axis=1)
                pre = jnp.stack([p_lo, p_hi + t_lo], axis=1)
            args.append(arg.reshape(rows, width))
            pre = pre.reshape(rows, width)
            tot = jnp.stack([both, both], axis=1).reshape(rows, width)
    return args, pre, tot


def _scan_blocks(nblk):
    fwd = lambda b, i: i
    bwd = lambda b, i: jnp.where(i == 0, 0, nblk - i)
    return fwd, bwd


def _hgrn_kernel(qf_ref, vf_ref, ff_ref, qb_ref, vb_ref, fb_ref, lb_ref, of_ref, ob_ref, st_ref):
    @pl.when(pl.program_id(1) == 0)
    def _():
        st_ref[...] = jnp.zeros_like(st_ref)

    levels = TILE.bit_length() - 1
    row = lax.broadcasted_iota(jnp.int32, (TILE, TILE), 0)
    colm = lax.broadcasted_iota(jnp.int32, (TILE, TILE), 1)
    eye = row == colm
    lvl_masks = []
    for j in range(levels):
        same = (row >> (j + 1)) == (colm >> (j + 1))
        t_up = (row & (1 << j)) != 0
        s_up = (colm & (1 << j)) != 0
        fwd_m = jnp.logical_and(same, jnp.logical_and(t_up, jnp.logical_not(s_up)))
        bwd_m = jnp.logical_and(same, jnp.logical_and(jnp.logical_not(t_up), s_up))
        lvl_masks.append((fwd_m, bwd_m))
    ridx = lax.broadcasted_iota(jnp.int32, (ROWS, GROUP_W), 0)

    for d, (q_ref, v_ref, f_ref, o_ref) in enumerate(((qf_ref, vf_ref, ff_ref, of_ref),
                                                      (qb_ref, vb_ref, fb_ref, ob_ref))):
        reverse = d == 1
        q = _silu(q_ref[0])
        v = v_ref[0]
        x = f_ref[0]
        log_lb = lb_ref[d:d + 1, :]
        log_1mlb = lb_ref[2 + d:3 + d, :]
        one_mlb = lb_ref[4 + d:5 + d, :]
        u = jnp.exp(-jnp.abs(x))
        b_term = log_1mlb + (jnp.minimum(x, 0.0) - jnp.log(1.0 + u))
        mx = jnp.maximum(log_lb, b_term)
        g = mx + jnp.log(1.0 + jnp.exp(-jnp.abs(log_lb - b_term)))
        k = one_mlb * (jnp.where(x >= 0.0, u, 1.0) / (1.0 + u))
        args, pre, tot = _pair_level_exponents(g, levels, reverse)
        q_in = (q * jnp.exp(pre)).astype(BF16)
        k_out = (k * jnp.exp(tot - pre)).astype(BF16)
        g_tile = jnp.exp(tot)
        qb16 = q.astype(BF16)
        kb16 = k.astype(BF16)
        vb16 = v.astype(BF16)
        qe, ke = [], []
        for j in range(levels):
            e = jnp.exp(args[j]).astype(BF16)
            qe.append(qb16 * e)
            ke.append(kb16 * e)
        ntile = ROWS // TILE
        for tix in (range(ntile - 1, -1, -1) if reverse else range(ntile)):
            rs = slice(tix * TILE, (tix + 1) * TILE)
            for h in range(N_HEADS):
                cs = slice(h * HEAD_DIM, (h + 1) * HEAD_DIM)
                attn = jnp.where(eye, _dot_nt(qb16[rs, cs], kb16[rs, cs]), 0.0)
                for j in range(levels):
                    attn = jnp.where(lvl_masks[j][d], _dot_nt(qe[j][rs, cs], ke[j][rs, cs]), attn)
                st = st_ref[d * N_HEADS + h]
                o = _dot(attn.astype(BF16), vb16[rs, cs]) + _dot_nt(q_in[rs, cs], st.astype(BF16))
                o_ref[0, rs, cs] = o
                upd = _dot_tn(vb16[rs, cs], k_out[rs, cs])
                st_ref[d * N_HEADS + h] = st * g_tile[tix * TILE:tix * TILE + 1, cs] + upd


def _hgrn(p, lb):
    nbatch, lt, _ = p.shape
    nblk = lt // ROWS
    fwd, bwd = _scan_blocks(nblk)

    def spec(blk, cb):
        return pl.BlockSpec((1, ROWS, GROUP_W), lambda b, i: (b, blk(b, i), cb))

    oshape = jax.ShapeDtypeStruct((nbatch, lt, GROUP_W), F32)
    return pl.pallas_call(
        _hgrn_kernel,
        out_shape=(oshape, oshape),
        grid=(nbatch, nblk),
        in_specs=[spec(fwd, HG_Q), spec(fwd, HG_I), spec(fwd, HG_FF),
                  spec(bwd, HG_Q), spec(bwd, HG_I), spec(bwd, HG_FB),
                  pl.BlockSpec((8, GROUP_W), lambda b, i: (0, 0))],
        out_specs=(spec(fwd, 0), spec(bwd, 0)),
        scratch_shapes=[pltpu.VMEM((2 * N_HEADS, HEAD_DIM, HEAD_DIM), F32)],
        compiler_params=_params("arbitrary", "arbitrary"), name="hgrn2_scan",
    )(p, p, p, p, p, p, lb)


def _gdn_kernel(qf_ref, kf_ref, vf_ref, sf_ref, qb_ref, kb_ref, vb_ref, sb_ref,
                ac_ref, of_ref, ob_ref, s_ref):
    @pl.when(pl.program_id(1) == 0)
    def _():
        s_ref[...] = jnp.zeros_like(s_ref)

    c = DN_CHUNK
    tb = DN_BLOCK
    levels = c.bit_length() - 1
    row = lax.broadcasted_iota(jnp.int32, (tb, tb), 0)
    colm = lax.broadcasted_iota(jnp.int32, (tb, tb), 1)
    same = (row >> levels) == (colm >> levels)
    eye = (row == colm).astype(F32)
    nchunk = ROWS // c

    chains, solves = [], []
    for d, refs in enumerate(((qf_ref, kf_ref, vf_ref, sf_ref), (qb_ref, kb_ref, vb_ref, sb_ref))):
        q_ref, k_ref, v_ref, sm_ref = refs
        reverse = d == 1
        sm = sm_ref[0]
        beta_c = _sigmoid(sm)
        la_c = -jnp.exp(ac_ref[0:1, :]) * _softplus(sm + ac_ref[1:2, :])
        cum_c, tot_c = _block_prefix(la_c, levels, reverse, 0)[levels]
        cum_r = cum_c.T
        incl = jnp.logical_and(same, (colm >= row) if reverse else (colm <= row))
        strict = jnp.logical_and(same, (colm > row) if reverse else (colm < row))
        q_all, k_all, v_all = q_ref[0], k_ref[0], v_ref[0]
        for h in range(N_HEADS):
            cs = slice(h * HEAD_DIM, (h + 1) * HEAD_DIM)
            bl = 4 * d + h
            al = 8 + 4 * d + h
            cc = cum_c[:, al:al + 1]
            tt = tot_c[:, al:al + 1]
            ch = dict(d=d, h=h, reverse=reverse, w=[], u=[], qk=[],
                      q_dec=(q_all[:, cs] * jnp.exp(cc)).astype(BF16),
                      k_dec=(k_all[:, cs] * jnp.exp(tt - cc)).astype(BF16),
                      g_tot=jnp.exp(tt), s=s_ref[d * N_HEADS + h], out=[None] * nchunk)
            chains.append(ch)
            for sb in range(ROWS // tb):
                rs = slice(sb * tb, (sb + 1) * tb)
                solves.append(dict(ch=ch, q=q_all[rs, cs], k=k_all[rs, cs], v=v_all[rs, cs], cc=cc[rs],
                                   cr=cum_r[al:al + 1, rs], bc=beta_c[rs, bl:bl + 1], incl=incl, strict=strict))
    for sv in solves:
        sv["decay"] = jnp.where(sv["incl"], jnp.exp(jnp.where(sv["incl"], sv["cc"] - sv["cr"], 0.0)), 0.0)
        sv["kb"] = sv["k"].astype(BF16)
    for sv in solves:
        sv["kk"] = _dot_nt(sv["kb"], sv["kb"])
    for sv in solves:
        sv["qk"] = _dot_nt(sv["q"].astype(BF16), sv["kb"])
    for sv in solves:
        sv["ch"]["qk"].append((sv["qk"] * sv["decay"]).astype(BF16))
        sv["pw"] = jnp.where(sv["strict"], sv["bc"] * sv["kk"] * sv["decay"], 0.0)
        sv["tinv"] = eye - sv["pw"]
    for _ in range(levels - 1):
        for sv in solves:
            pwb = sv["pw"].astype(BF16)
            sv["pw"] = _dot(pwb, pwb)
        for sv in solves:
            sv["tinv"] = sv["tinv"] + _dot(sv["tinv"].astype(BF16), sv["pw"].astype(BF16))
    for sv in solves:
        rhs = jnp.concatenate([sv["k"] * (sv["bc"] * jnp.exp(sv["cc"])), sv["v"] * sv["bc"]], axis=1)
        sv["wu"] = _dot(sv["tinv"].astype(BF16), rhs.astype(BF16))
    for sv in solves:
        sv["ch"]["w"].append(sv["wu"][:, :HEAD_DIM])
        sv["ch"]["u"].append(sv["wu"][:, HEAD_DIM:])

    per = tb // c
    for step in range(nchunk):
        for ch in chains:
            ci = nchunk - 1 - step if ch["reverse"] else step
            ch["ci"], ch["rs"] = ci, slice(ci * c, (ci + 1) * c)
            ch["sbi"] = ci // per
            ch["sub"] = slice((ci % per) * c, (ci % per) * c + c)
            wq = jnp.concatenate([ch["w"][ch["sbi"]][ch["sub"]].astype(BF16), ch["q_dec"][ch["rs"]]], axis=0)
            ch["ws"] = _dot(wq, ch["s"].astype(BF16))
        for ch in chains:
            ch["v_new"] = (ch["u"][ch["sbi"]][ch["sub"]] - ch["ws"][:c]).astype(BF16)
            ch["upd"] = _dot_tn(ch["k_dec"][ch["rs"]], ch["v_new"])
        for ch in chains:
            ci = ch["ci"]
            ch["out"][ci] = ch["ws"][c:] + _dot(ch["qk"][ch["sbi"]][ch["sub"], ch["sub"]], ch["v_new"])
            ch["s"] = ch["g_tot"][ci * c:ci * c + 1] * ch["s"] + ch["upd"]

    for d, o_ref in enumerate((of_ref, ob_ref)):
        mine = [ch for ch in chains if ch["d"] == d]
        o_ref[0] = jnp.concatenate([jnp.concatenate(ch["out"], axis=0) for ch in mine], axis=1)
        for ch in mine:
            s_ref[d * N_HEADS + ch["h"]] = ch["s"]


def _gdn(qkv, p, acoef_c):
    nbatch, lt, _ = qkv.shape
    nblk = lt // ROWS
    fwd, bwd = _scan_blocks(nblk)

    def spec(blk, cb):
        return pl.BlockSpec((1, ROWS, GROUP_W), lambda b, i: (b, blk(b, i), cb))

    def small(blk):
        return pl.BlockSpec((1, ROWS, LANES), lambda b, i: (b, blk(b, i), SMALL_BLK))

    oshape = jax.ShapeDtypeStruct((nbatch, lt, GROUP_W), F32)
    return pl.pallas_call(
        _gdn_kernel,
        out_shape=(oshape, oshape),
        grid=(nbatch, nblk),
        in_specs=[spec(fwd, 0), spec(fwd, 1), spec(fwd, 2), small(fwd),
                  spec(bwd, 0), spec(bwd, 1), spec(bwd, 2), small(bwd),
                  pl.BlockSpec((8, LANES), lambda b, i: (0, 0))],
        out_specs=(spec(fwd, 0), spec(bwd, 0)),
        scratch_shapes=[pltpu.VMEM((2 * N_HEADS, HEAD_DIM, HEAD_DIM), F32)],
        compiler_params=_params("arbitrary", "arbitrary"), name="gated_deltanet_scan",
    )(qkv, qkv, qkv, p, qkv, qkv, qkv, p, acoef_c)


def _outproj_kernel(x_ref, hf_ref, hb_ref, df_ref, db_ref, g_ref, z_ref, m_ref, hw_ref, dw_ref,
                    wo_ref, n2_ref, wr_ref, br_ref, xo_ref, h2_ref, cnt_ref, carry_ref):
    parts = []
    for (a_ref, b_ref, gate_ref, nw_ref) in ((hf_ref, hb_ref, g_ref, hw_ref), (df_ref, db_ref, z_ref, dw_ref)):
        o = a_ref[0] + b_ref[0]
        gate = gate_ref[0]
        for h in range(N_HEADS):
            cs = slice(h * HEAD_DIM, (h + 1) * HEAD_DIM)
            parts.append((_rms(o[:, cs], nw_ref[...]) * _silu(gate[:, cs])).astype(BF16))
    y = jnp.concatenate(parts, axis=1)
    x1 = x_ref[0] + m_ref[0, 2:3, :] * _dot(y, wo_ref[...])
    xo_ref[0] = x1
    h2 = _rms(x1, n2_ref[...]) * (1.0 + m_ref[0, 4:5, :]) + m_ref[0, 3:4, :]
    h2_ref[0, :, :D_MODEL] = h2
    h_hi = h2.astype(BF16)
    h_lo = (h2 - h_hi.astype(F32)).astype(BF16)
    wr = wr_ref[...]
    w_hi = wr.astype(BF16)
    w_lo = (wr - w_hi.astype(F32)).astype(BF16)
    logits = _dot(h_hi, w_hi) + (_dot(h_hi, w_lo) + _dot(h_lo, w_hi)) + br_ref[...]
    cls, gate_a, gate_b = _route(logits)

    @pl.when(jnp.logical_and(pl.program_id(0) == 0, pl.program_id(1) == 0))
    def _():
        carry_ref[...] = jnp.zeros_like(carry_ref)

    lane = lax.broadcasted_iota(jnp.int32, (ROWS, LANES), 1).astype(F32)
    onehot = jnp.where(lane == cls, 1.0, 0.0)
    r_i = lax.broadcasted_iota(jnp.int32, (ROWS, ROWS), 0)
    c_i = lax.broadcasted_iota(jnp.int32, (ROWS, ROWS), 1)
    earlier = jnp.where(c_i < r_i, 1.0, 0.0).astype(BF16)
    before = _dot(earlier, onehot.astype(BF16))
    carry = carry_ref[0:1, :]
    rank = jnp.sum(onehot * (before + carry), axis=-1, keepdims=True)
    carry = carry + jnp.sum(onehot, axis=0, keepdims=True)
    carry_ref[0:1, :] = carry
    cnt_ref[...] = jnp.broadcast_to(carry, cnt_ref.shape)
    h2_ref[0, :, D_MODEL:] = (jnp.where(lane == float(ROUTE_CLS), cls, 0.0)
                              + jnp.where(lane == float(ROUTE_GATE_A), gate_a, 0.0)
                              + jnp.where(lane == float(ROUTE_GATE_B), gate_b, 0.0)
                              + jnp.where(lane == float(ROUTE_RANK), rank, 0.0))


def _outproj(x, ohf, ohb, odf, odb, p, mod, hg_nw, dn_nw, w_out, norm2_w, w_route, b_route):
    nbatch, lt, d = x.shape
    grid = (nbatch, lt // ROWS)
    xspec = pl.BlockSpec((1, ROWS, d), lambda b, i: (b, i, 0))
    ospec = pl.BlockSpec((1, ROWS, GROUP_W), lambda b, i: (b, i, 0))

    def pcol(cb):
        return pl.BlockSpec((1, ROWS, GROUP_W), lambda b, i: (b, i, cb))

    def full(shape):
        return pl.BlockSpec(shape, lambda b, i: tuple(0 for _ in shape))

    return pl.pallas_call(
        _outproj_kernel,
        out_shape=(jax.ShapeDtypeStruct(x.shape, F32), jax.ShapeDtypeStruct((nbatch, lt, d + LANES), F32),
                   jax.ShapeDtypeStruct((8, LANES), F32)),
        grid=grid,
        in_specs=[xspec, ospec, ospec, ospec, ospec, pcol(HG_G), pcol(DN_Z),
                  pl.BlockSpec((1, 6, d), _mod_row(nbatch)),
                  full((1, HEAD_DIM)), full((1, HEAD_DIM)), full((d, d)), full((1, d)),
                  full((d, LANES)), full((1, LANES))],
        out_specs=(xspec, pl.BlockSpec((1, ROWS, d + LANES), lambda b, i: (b, i, 0)), full((8, LANES))),
        scratch_shapes=[pltpu.VMEM((8, LANES), F32)],
        compiler_params=_params("arbitrary", "arbitrary"), name="out_proj",
    )(x, ohf, ohb, odf, odb, p, p, mod, hg_nw, dn_nw, w_out, norm2_w, w_route, b_route)


EXPERT_TILE = 256
MOVE_TILE = 512


def _plan(cls, rank, counts):
    t = cls.shape[0]
    tiles_c = (counts + EXPERT_TILE - 1) // EXPERT_TILE
    tile_end = jnp.cumsum(tiles_c)
    tile_start = tile_end - tiles_c
    pos = tile_start[cls] * EXPERT_TILE + rank
    ntiles = t // EXPERT_TILE + N_CLASSES
    tile_id = jnp.arange(ntiles, dtype=jnp.int32)
    tile_cls = jnp.minimum(jnp.sum((tile_id[:, None] >= tile_end[None, :]).astype(jnp.int32), axis=1),
                           N_CLASSES - 1)
    used = (tile_id < tile_end[-1]).astype(jnp.int32)
    pair_a = jnp.array([a for a, _ in PAIRS], jnp.int32)
    pair_b = jnp.array([b for _, b in PAIRS], jnp.int32)
    group = tile_cls // len(PAIRS)
    ea = group * EXPERTS_PER_GROUP + pair_a[tile_cls % len(PAIRS)]
    eb = group * EXPERTS_PER_GROUP + pair_b[tile_cls % len(PAIRS)]
    return pos, ntiles, ea, eb, used


def _move_kernel(scatter, idx_ref, src_ref, *rest):
    dst_ref, sems = rest[-2], rest[-1]
    i = pl.program_id(0)
    n = pl.num_programs(0)
    rows = idx_ref.shape[2]
    slot = i % 2

    def wait_step(s):
        pltpu.make_async_copy(src_ref.at[pl.ds(0, rows), :], dst_ref.at[pl.ds(0, rows), :], sems.at[s]).wait()

    def issue(r, carry):
        here = i * rows + r
        there = idx_ref[0, 0, r]
        s_row, d_row = (here, there) if scatter else (there, here)
        pltpu.make_async_copy(src_ref.at[pl.ds(s_row, 1), :], dst_ref.at[pl.ds(d_row, 1), :], sems.at[slot]).start()
        return carry

    lax.fori_loop(0, rows, issue, 0, unroll=8)

    @pl.when(i > 0)
    def _():
        wait_step(1 - slot)

    @pl.when(i == n - 1)
    def _():
        wait_step(slot)


def _move_rows(src, idx, n_out, scatter):
    n = idx.shape[0]
    d = src.shape[1]
    tm = MOVE_TILE
    idx_spec = pl.BlockSpec((1, 1, tm), lambda i: (i, 0, 0), memory_space=pltpu.SMEM)
    hbm = pl.BlockSpec(memory_space=pl.ANY)
    common = dict(out_shape=jax.ShapeDtypeStruct((n_out, d), src.dtype), grid=(n // tm,), out_specs=hbm,
                  scratch_shapes=[pltpu.SemaphoreType.DMA((2,))], compiler_params=_params("arbitrary"))
    idx3 = idx.reshape(n // tm, 1, tm)
    if scatter:
        return pl.pallas_call(functools.partial(_move_kernel, True), in_specs=[idx_spec, hbm, hbm],
                              input_output_aliases={2: 0}, name="scatter_rows", **common
                              )(idx3, src, jnp.zeros((n_out, d), src.dtype))
    return pl.pallas_call(functools.partial(_move_kernel, False), in_specs=[idx_spec, hbm],
                          name="gather_rows", **common)(idx3, src)


def _expert_kernel(ea_ref, eb_ref, used_ref, x_ref, wga_ref, wua_ref, wda_ref, wgb_ref, wub_ref, wdb_ref, o_ref):
    i = pl.program_id(0)

    @pl.when(used_ref[i] != 0)
    def _():
        x = x_ref[:, :D_MODEL].astype(BF16)
        gate_a = x_ref[:, D_MODEL + ROUTE_GATE_A:D_MODEL + ROUTE_GATE_A + 1]
        gate_b = x_ref[:, D_MODEL + ROUTE_GATE_B:D_MODEL + ROUTE_GATE_B + 1]
        act_a = _silu(_dot(x, wga_ref[0])) * _dot(x, wua_ref[0]) * gate_a
        act_b = _silu(_dot(x, wgb_ref[0])) * _dot(x, wub_ref[0]) * gate_b
        o_ref[...] = _dot(act_a.astype(BF16), wda_ref[0]) + _dot(act_b.astype(BF16), wdb_ref[0])

    @pl.when(used_ref[i] == 0)
    def _():
        o_ref[...] = jnp.zeros_like(o_ref)


def _experts(xs_sorted, ea, eb, used, w_gate, w_up, w_down):
    n, dx = xs_sorted.shape
    _, d, ff = w_gate.shape
    tm = EXPERT_TILE
    wa = lambda i, ea, eb, used: (ea[i], 0, 0)
    wb = lambda i, ea, eb, used: (eb[i], 0, 0)
    row = lambda i, ea, eb, used: (i, 0)
    return pl.pallas_call(
        _expert_kernel,
        out_shape=jax.ShapeDtypeStruct((n, d), F32),
        grid_spec=pltpu.PrefetchScalarGridSpec(
            num_scalar_prefetch=3,
            grid=(n // tm,),
            in_specs=[pl.BlockSpec((tm, dx), row),
                      pl.BlockSpec((1, d, ff), wa), pl.BlockSpec((1, d, ff), wa), pl.BlockSpec((1, ff, d), wa),
                      pl.BlockSpec((1, d, ff), wb), pl.BlockSpec((1, d, ff), wb), pl.BlockSpec((1, ff, d), wb)],
            out_specs=pl.BlockSpec((tm, d), row)),
        compiler_params=_params("arbitrary"), name="moe_experts",
    )(ea, eb, used, xs_sorted, w_gate, w_up, w_down, w_gate, w_up, w_down)


def _moe(h2ext, counts, w_gate, w_up, w_down):
    d = D_MODEL
    cls = h2ext[:, d + ROUTE_CLS].astype(jnp.int32)
    rank = h2ext[:, d + ROUTE_RANK].astype(jnp.int32)
    pos, ntiles, ea, eb, used = _plan(cls, rank, counts)
    xs_sorted = _move_rows(h2ext, pos, ntiles * EXPERT_TILE, scatter=True)
    ys_sorted = _experts(xs_sorted, ea, eb, used, w_gate, w_up, w_down)
    return _move_rows(ys_sorted, pos, h2ext.shape[0], scatter=False)


def _route(lg):
    lane = lax.broadcasted_iota(jnp.int32, lg.shape, 1).astype(F32)
    neg = -jnp.inf
    big = float(LANES)

    def first_argmax(vals):
        m = jnp.max(vals, axis=-1, keepdims=True)
        return m, jnp.min(jnp.where(vals == m, lane, big), axis=-1, keepdims=True)

    in_groups = lane < N_GROUPS
    gl = jnp.where(in_groups, lg, neg)
    gmax, gidx = first_argmax(gl)
    g_w = 1.0 / jnp.sum(jnp.where(in_groups, jnp.exp(gl - gmax), 0.0), axis=-1, keepdims=True)
    lo = ROUTE_OFF + EXPERTS_PER_GROUP * gidx
    el = jnp.where(jnp.logical_and(lane >= lo, lane < lo + EXPERTS_PER_GROUP), lg, neg)
    m1, i1 = first_argmax(el)
    el2 = jnp.where(lane == i1, neg, el)
    m2, i2 = first_argmax(el2)
    e2 = jnp.exp(m2 - m1)
    w1 = g_w / (1.0 + e2)
    w2 = g_w * e2 / (1.0 + e2)
    first_lower = i1 < i2
    a = jnp.minimum(i1, i2) - lo
    b = jnp.maximum(i1, i2) - lo
    pair = a * (EXPERTS_PER_GROUP - 1) - a * (a - 1.0) * 0.5 + (b - a - 1.0)
    cls = gidx * float(len(PAIRS)) + pair
    return cls, jnp.where(first_lower, w1, w2), jnp.where(first_lower, w2, w1)


def _final_kernel(x_ref, y_ref, m_ref, nw_ref, o_ref):
    x = x_ref[0] + m_ref[0, 5:6, :] * y_ref[0]
    o_ref[0] = _rms(x, nw_ref[...])


def _final(x, y, mod, norm_w, ctx_blocks):
    nbatch, lt, d = x.shape
    nlat = lt // ROWS - ctx_blocks
    xspec = pl.BlockSpec((1, ROWS, d), lambda b, i: (b, i + ctx_blocks, 0))
    return pl.pallas_call(
        _final_kernel,
        out_shape=jax.ShapeDtypeStruct((nbatch, nlat * ROWS, d), F32),
        grid=(nbatch, nlat),
        in_specs=[xspec, xspec, pl.BlockSpec((1, 6, d), lambda b, i: (b, 0, 0)),
                  pl.BlockSpec((1, d), lambda b, i: (0, 0))],
        out_specs=pl.BlockSpec((1, ROWS, d), lambda b, i: (b, i, 0)),
        compiler_params=_params("arbitrary", "arbitrary"), name="final_norm",
    )(x, y, mod, norm_w)


def kernel(x, c, ctx, c_ctx, w_mod, b_mod, norm1_w, norm2_w, w_in, w_out, hg_lb, hg_norm_w, dn_conv_w,
           dn_a_log, dn_dt_bias, dn_norm_w, w_group, b_group, w_expert, b_expert, w_gate, w_up, w_down,
           final_norm_w):
    nbatch, seq, d = x.shape
    ctx_len = ctx.shape[1]
    depth = w_mod.shape[0]
    assert d == D_MODEL and ctx_len == ROWS and seq % ROWS == 0 and ROWS % GRID_W == 0
    assert w_in.shape[2] == IN_COLS
    lt = ctx_len + seq

    nb = -(-(nbatch + 1) // 8) * 8
    cvec = jnp.concatenate([c, c_ctx[None, :], jnp.zeros((nb - nbatch - 1, d), F32)], axis=0)
    mods = _modulation(cvec, w_mod, b_mod).reshape(depth, nb, 6, d)
    lbs = _lower_bounds(hg_lb)

    c_dn = 5 * GROUP_W
    w_qkv = w_in[:, :, c_dn:c_dn + 3 * GROUP_W].astype(BF16)
    w_rest = jnp.concatenate([w_in[:, :, :c_dn], w_in[:, :, c_dn + 3 * GROUP_W:],
                              jnp.zeros((depth, d, REST_COLS + 3 * GROUP_W - IN_COLS), F32)], axis=2).astype(BF16)
    w_out_b = w_out.astype(BF16)
    w_gate_b, w_up_b, w_down_b = w_gate.astype(BF16), w_up.astype(BF16), w_down.astype(BF16)
    nroute = N_GROUPS + N_EXPERTS
    w_route = jnp.pad(jnp.concatenate([w_group, w_expert], axis=2), ((0, 0), (0, 0), (0, LANES - nroute)))
    b_route = jnp.pad(jnp.concatenate([b_group, b_expert], axis=1), ((0, 0), (0, LANES - nroute)))
    conv_w = dn_conv_w.reshape(depth, 9, 3 * GROUP_W)
    a_flat = dn_a_log.reshape(depth, 2 * N_HEADS)
    dt_flat = dn_dt_bias.reshape(depth, 2 * N_HEADS)
    lane_vals = lambda v: jnp.pad(v, ((0, 0), (2 * N_HEADS, LANES - 4 * N_HEADS)))
    acoef_c = jnp.pad(jnp.stack([lane_vals(a_flat), lane_vals(dt_flat)], axis=1), ((0, 0), (0, 6), (0, 0)))

    xs = jnp.concatenate([ctx, x], axis=1)
    y = None
    for l in range(depth):
        xs, qkv, p = _inproj(xs, y, mods[l - 1] if l else None, mods[l], norm1_w[l][None, :],
                             w_qkv[l], w_rest[l], conv_w[l])
        ohf, ohb = _hgrn(p, lbs[l])
        odf, odb = _gdn(qkv, p, acoef_c[l])
        xs, h2ext, counts = _outproj(xs, ohf, ohb, odf, odb, p, mods[l], hg_norm_w[l][None, :],
                                     dn_norm_w[l][None, :], w_out_b[l], norm2_w[l][None, :],
                                     w_route[l], b_route[l][None, :])
        y = _moe(h2ext.reshape(nbatch * lt, d + LANES), counts[0, :N_CLASSES].astype(jnp.int32),
                 w_gate_b[l], w_up_b[l], w_down_b[l]).reshape(nbatch, lt, d)
    return _final(xs, y, mods[depth - 1], final_norm_w[None, :], ctx_len // ROWS)
```

```python
import functools

import jax
import jax.numpy as jnp
from jax import lax
from jax.experimental import pallas as pl
from jax.experimental.pallas import tpu as pltpu

F32 = jnp.float32
BF16 = jnp.bfloat16

D_MODEL = 1024
N_HEADS = 4
HEAD_DIM = 128
GROUP_W = N_HEADS * HEAD_DIM
GRID_W = 64
N_GROUPS = 4
EXPERTS_PER_GROUP = 4
N_EXPERTS = N_GROUPS * EXPERTS_PER_GROUP
EXPERT_FF = 512
NORM_EPS = 1e-6
LANES = 128
IN_COLS = 9 * GROUP_W + 4 * N_HEADS
REST_COLS = 6 * GROUP_W + LANES
SMALL_BLK = 6 * GROUP_W // LANES
ROWS = 256
TILE = 128
DN_CHUNK = 64
DN_BLOCK = 128
ROUTE_OFF = N_GROUPS
PAIRS = [(a, b) for a in range(EXPERTS_PER_GROUP) for b in range(a + 1, EXPERTS_PER_GROUP)]
N_CLASSES = N_GROUPS * len(PAIRS)
ROUTE_CLS, ROUTE_GATE_A, ROUTE_GATE_B, ROUTE_RANK = 0, 1, 2, 3
VMEM_LIMIT = 56 * 1024 * 1024

HG_Q, HG_I, HG_G, HG_FF, HG_FB, DN_Z = range(6)


def _sigmoid(x):
    return 1.0 / (1.0 + jnp.exp(-x))


def _silu(x):
    return x * _sigmoid(x)


def _softplus(x):
    return jnp.maximum(x, 0.0) + jnp.log(1.0 + jnp.exp(-jnp.abs(x)))


def _dot(a, b):
    return jnp.dot(a, b, preferred_element_type=F32)


def _dot_nt(a, b):
    return lax.dot_general(a, b, (((1,), (1,)), ((), ())), preferred_element_type=F32)


def _dot_tn(a, b):
    return lax.dot_general(a, b, (((0,), (0,)), ((), ())), preferred_element_type=F32)


def _rms(x, w):
    return x * lax.rsqrt(jnp.mean(x * x, axis=-1, keepdims=True) + NORM_EPS) * w


def _params(*sem):
    return pltpu.CompilerParams(dimension_semantics=sem, vmem_limit_bytes=VMEM_LIMIT)


def _mod_kernel(c_ref, w_ref, b_ref, o_ref):
    a = _silu(c_ref[...])
    o_ref[0] = jnp.dot(a, w_ref[0], preferred_element_type=F32,
                       precision=lax.Precision.HIGHEST) + b_ref[0]


def _modulation(cvec, w_mod, b_mod):
    depth, d, n = w_mod.shape
    nb = cvec.shape[0]
    tn = 1536
    return pl.pallas_call(
        _mod_kernel,
        out_shape=jax.ShapeDtypeStruct((depth, nb, n), F32),
        grid=(depth, n // tn),
        in_specs=[pl.BlockSpec((nb, d), lambda l, j: (0, 0)),
                  pl.BlockSpec((1, d, tn), lambda l, j: (l, 0, j)),
                  pl.BlockSpec((1, 1, tn), lambda l, j: (l, 0, j))],
        out_specs=pl.BlockSpec((1, nb, tn), lambda l, j: (l, 0, j)),
        compiler_params=_params("arbitrary", "arbitrary"),
        name="adaln_modulation",
    )(cvec, w_mod, b_mod.reshape(depth, 1, n))


def _lb_kernel(depth, lb_ref, o_ref):
    x = [lb_ref[l] for l in range(depth)]
    m = functools.reduce(jnp.maximum, x)
    e = [jnp.exp(v - m) for v in x]
    tot = functools.reduce(lambda a, b: a + b, e)
    sm = [v / tot for v in e]
    cum = sm[0]
    first = cum
    for l in range(depth):
        if l > 0:
            cum = cum + sm[l]
        lb = cum - first
        o_ref[l, 0:2, :] = jnp.log(lb)
        o_ref[l, 2:4, :] = jnp.log1p(-lb)
        o_ref[l, 4:6, :] = 1.0 - lb
        o_ref[l, 6:8, :] = jnp.zeros_like(lb)


def _lower_bounds(hg_lb):
    depth = hg_lb.shape[0]
    return pl.pallas_call(
        functools.partial(_lb_kernel, depth),
        out_shape=jax.ShapeDtypeStruct((depth, 8, GROUP_W), F32),
        name="hgrn_lower_bounds",
    )(hg_lb)


def _short_conv(xe, w_ref, cs, is_ctx):
    n_ext = xe.shape[0]
    col = lax.broadcasted_iota(jnp.int32, (n_ext, 1), 0) % GRID_W
    acc = None
    for dw in (-1, 0, 1):
        if dw == 0:
            xs = xe
        else:
            xs = pltpu.roll(xe, (-dw) % n_ext, 0)
            inside = jnp.logical_and(col + dw >= 0, col + dw < GRID_W)
            xs = jnp.where(jnp.logical_or(is_ctx, inside), xs, 0.0)
        for dh in (-1, 0, 1):
            tap = (dh + 1) * 3 + (dw + 1)
            w = w_ref[tap:tap + 1, cs]
            if dh != 0:
                w = jnp.where(is_ctx, 0.0, w)
            term = xs[GRID_W * (1 + dh):GRID_W * (1 + dh) + ROWS, :] * w
            acc = term if acc is None else acc + term
    return acc


def _inproj_kernel(has_y, nblk, *refs):
    if has_y:
        (x_ref, xp_ref, xn_ref, y_ref, yp_ref, yn_ref, pm_ref, m_ref, nw_ref, wq_ref, wr_ref, cw_ref,
         xo_ref, qkv_ref, p_ref) = refs
        g2 = pm_ref[0, 5:6, :]
        x = x_ref[0] + g2 * y_ref[0]
        xo_ref[0] = x
        x_prev = xp_ref[0] + g2 * yp_ref[0]
        x_next = xn_ref[0] + g2 * yn_ref[0]
    else:
        x_ref, xp_ref, xn_ref, m_ref, nw_ref, wq_ref, wr_ref, cw_ref, qkv_ref, p_ref = refs
        x, x_prev, x_next = x_ref[0], xp_ref[0], xn_ref[0]
    i = pl.program_id(1)
    is_ctx = i == 0
    prev_ok = i >= 2
    next_ok = jnp.logical_and(i >= 1, i < nblk - 1)
    xe = jnp.concatenate([x_prev, x, x_next], axis=0)
    he = (_rms(xe, nw_ref[...]) * (1.0 + m_ref[0, 1:2, :]) + m_ref[0, 0:1, :]).astype(BF16)
    pq = _dot(he, wq_ref[...])
    row = lax.broadcasted_iota(jnp.int32, (ROWS + 2 * GRID_W, 1), 0)
    halo_ok = jnp.logical_and(jnp.logical_or(row >= GRID_W, prev_ok),
                              jnp.logical_or(row < GRID_W + ROWS, next_ok))
    pq = jnp.where(halo_ok, pq, 0.0)
    p_ref[0] = _dot(he[GRID_W:GRID_W + ROWS], wr_ref[...])
    for sec in range(3):
        for h in range(N_HEADS):
            cs = slice(sec * GROUP_W + h * HEAD_DIM, sec * GROUP_W + (h + 1) * HEAD_DIM)
            yh = _silu(_short_conv(pq[:, cs], cw_ref, cs, is_ctx))
            if sec < 2:
                yh = yh * lax.rsqrt(jnp.sum(yh * yh, axis=-1, keepdims=True) + NORM_EPS)
                if sec == 0:
                    yh = yh * HEAD_DIM ** -0.5
            qkv_ref[0, :, cs] = yh


def _mod_row(nbatch):
    return lambda b, i: (jnp.where(i == 0, nbatch, b), 0, 0)


def _inproj(x, y, prev_mod, mod, norm_w, w_qkv, w_rest, conv_w):
    nbatch, lt, d = x.shape
    nblk = lt // ROWS
    hpb = ROWS // GRID_W
    nhalo = lt // GRID_W
    nrest = w_rest.shape[1]
    nqkv = w_qkv.shape[1]
    grid = (nbatch, nblk)
    xspec = pl.BlockSpec((1, ROWS, d), lambda b, i: (b, i, 0))
    prev = pl.BlockSpec((1, GRID_W, d), lambda b, i: (b, jnp.maximum(i * hpb - 1, 0), 0))
    nxt = pl.BlockSpec((1, GRID_W, d), lambda b, i: (b, jnp.minimum((i + 1) * hpb, nhalo - 1), 0))
    mspec = pl.BlockSpec((1, 6, d), _mod_row(nbatch))

    def full(shape):
        return pl.BlockSpec(shape, lambda b, i: tuple(0 for _ in shape))

    wspecs = [full((1, d)), full((d, nqkv)), full((d, nrest)), full((9, nqkv))]
    outs = (jax.ShapeDtypeStruct((nbatch, lt, nqkv), F32), jax.ShapeDtypeStruct((nbatch, lt, nrest), F32))
    ospecs = (pl.BlockSpec((1, ROWS, nqkv), lambda b, i: (b, i, 0)),
              pl.BlockSpec((1, ROWS, nrest), lambda b, i: (b, i, 0)))
    if y is None:
        qkv, p = pl.pallas_call(
            functools.partial(_inproj_kernel, False, nblk),
            out_shape=outs, grid=grid,
            in_specs=[xspec, prev, nxt, mspec] + wspecs, out_specs=ospecs,
            compiler_params=_params("arbitrary", "arbitrary"), name="in_proj",
        )(x, x, x, mod, norm_w, w_qkv, w_rest, conv_w)
        return x, qkv, p
    return pl.pallas_call(
        functools.partial(_inproj_kernel, True, nblk),
        out_shape=(jax.ShapeDtypeStruct(x.shape, F32),) + outs, grid=grid,
        in_specs=[xspec, prev, nxt, xspec, prev, nxt, mspec, mspec] + wspecs, out_specs=(xspec,) + ospecs,
        compiler_params=_params("arbitrary", "arbitrary"), name="in_proj_res",
    )(x, x, x, y, y, y, prev_mod, mod, norm_w, w_qkv, w_rest, conv_w)


def _block_prefix(g, levels, reverse, axis):
    n = g.shape[axis]
    idx = lax.broadcasted_iota(jnp.int32, g.shape, axis)
    pre, tot = g, g
    out = []
    for j in range(levels):
        out.append((pre, tot))
        half = 1 << j
        upper = (idx & half) != 0
        below = pltpu.roll(tot, half, axis)
        above = pltpu.roll(tot, n - half, axis)
        if reverse:
            pre = pre + jnp.where(upper, 0.0, above)
        else:
            pre = pre + jnp.where(upper, below, 0.0)
        tot = tot + jnp.where(upper, below, above)
    out.append((pre, tot))
    return out


def _pair_level_exponents(g, levels, reverse):
    rows, width = g.shape
    sub = 8
    idx = lax.broadcasted_iota(jnp.int32, g.shape, 0)
    pre, tot = g, g
    args = []
    for j in range(levels):
        half = 1 << j
        if half < sub:
            upper = (idx & half) != 0
            q_side = jnp.logical_not(upper) if reverse else upper
            args.append(jnp.where(q_side, pre, tot - pre))
            below = pltpu.roll(tot, half, 0)
            above = pltpu.roll(tot, rows - half, 0)
            if reverse:
                pre = pre + jnp.where(upper, 0.0, above)
            else:
                pre = pre + jnp.where(upper, below, 0.0)
            tot = tot + jnp.where(upper, below, above)
        else:
            nb = rows // (2 * half)
            p4 = pre.reshape(nb, 2, half, width)
            t4 = tot.reshape(nb, 2, half, width)
            p_lo, p_hi, t_lo, t_hi = p4[:, 0], p4[:, 1], t4[:, 0], t4[:, 1]
            both = t_lo + t_hi
            if reverse:
                arg = jnp.stack([p_lo, t_hi - p_hi], axis=1)
                pre = jnp.stack([p_lo + t_hi, p_hi], axis=1)
            else:
                arg = jnp.stack([t_lo - p_lo, p_hi], axis=1)
                pre = jnp.stack([p_lo, p_hi + t_lo], axis=1)
            args.append(arg.reshape(rows, width))
            pre = pre.reshape(rows, width)
            tot = jnp.stack([both, both], axis=1).reshape(rows, width)
    return args, pre, tot


def _scan_blocks(nblk):
    fwd = lambda b, i: i
    bwd = lambda b, i: jnp.where(i == 0, 0, nblk - i)
    return fwd, bwd


def _hgrn_kernel(qf_ref, vf_ref, ff_ref, qb_ref, vb_ref, fb_ref, lb_ref, of_ref, ob_ref, st_ref):
    @pl.when(pl.program_id(1) == 0)
    def _():
        st_ref[...] = jnp.zeros_like(st_ref)

    levels = TILE.bit_length() - 1
    row = lax.broadcasted_iota(jnp.int32, (TILE, TILE), 0)
    colm = lax.broadcasted_iota(jnp.int32, (TILE, TILE), 1)
    eye = row == colm
    lvl_masks = []
    for j in range(levels):
        same = (row >> (j + 1)) == (colm >> (j + 1))
        t_up = (row & (1 << j)) != 0
        s_up = (colm & (1 << j)) != 0
        fwd_m = jnp.logical_and(same, jnp.logical_and(t_up, jnp.logical_not(s_up)))
        bwd_m = jnp.logical_and(same, jnp.logical_and(jnp.logical_not(t_up), s_up))
        lvl_masks.append((fwd_m, bwd_m))
    ridx = lax.broadcasted_iota(jnp.int32, (ROWS, GROUP_W), 0)

    for d, (q_ref, v_ref, f_ref, o_ref) in enumerate(((qf_ref, vf_ref, ff_ref, of_ref),
                                                      (qb_ref, vb_ref, fb_ref, ob_ref))):
        reverse = d == 1
        q = _silu(q_ref[0])
        v = v_ref[0]
        x = f_ref[0]
        log_lb = lb_ref[d:d + 1, :]
        log_1mlb = lb_ref[2 + d:3 + d, :]
        one_mlb = lb_ref[4 + d:5 + d, :]
        u = jnp.exp(-jnp.abs(x))
        b_term = log_1mlb + (jnp.minimum(x, 0.0) - jnp.log(1.0 + u))
        mx = jnp.maximum(log_lb, b_term)
        g = mx + jnp.log(1.0 + jnp.exp(-jnp.abs(log_lb - b_term)))
        k = one_mlb * (jnp.where(x >= 0.0, u, 1.0) / (1.0 + u))
        args, pre, tot = _pair_level_exponents(g, levels, reverse)
        q_in = (q * jnp.exp(pre)).astype(BF16)
        k_out = (k * jnp.exp(tot - pre)).astype(BF16)
        g_tile = jnp.exp(tot)
        qb16 = q.astype(BF16)
        kb16 = k.astype(BF16)
        vb16 = v.astype(BF16)
        qe, ke = [], []
        for j in range(levels):
            e = jnp.exp(args[j]).astype(BF16)
            qe.append(qb16 * e)
            ke.append(kb16 * e)
        ntile = ROWS // TILE
        for tix in (range(ntile - 1, -1, -1) if reverse else range(ntile)):
            rs = slice(tix * TILE, (tix + 1) * TILE)
            for h in range(N_HEADS):
                cs = slice(h * HEAD_DIM, (h + 1) * HEAD_DIM)
                attn = jnp.where(eye, _dot_nt(qb16[rs, cs], kb16[rs, cs]), 0.0)
                for j in range(levels):
                    attn = jnp.where(lvl_masks[j][d], _dot_nt(qe[j][rs, cs], ke[j][rs, cs]), attn)
                st = st_ref[d * N_HEADS + h]
                o = _dot(attn.astype(BF16), vb16[rs, cs]) + _dot_nt(q_in[rs, cs], st.astype(BF16))
                o_ref[0, rs, cs] = o
                upd = _dot_tn(vb16[rs, cs], k_out[rs, cs])
                st_ref[d * N_HEADS + h] = st * g_tile[tix * TILE:tix * TILE + 1, cs] + upd


def _hgrn(p, lb):
    nbatch, lt, _ = p.shape
    nblk = lt // ROWS
    fwd, bwd = _scan_blocks(nblk)

    def spec(blk, cb):
        return pl.BlockSpec((1, ROWS, GROUP_W), lambda b, i: (b, blk(b, i), cb))

    oshape = jax.ShapeDtypeStruct((nbatch, lt, GROUP_W), F32)
    return pl.pallas_call(
        _hgrn_kernel,
        out_shape=(oshape, oshape),
        grid=(nbatch, nblk),
        in_specs=[spec(fwd, HG_Q), spec(fwd, HG_I), spec(fwd, HG_FF),
                  spec(bwd, HG_Q), spec(bwd, HG_I), spec(bwd, HG_FB),
                  pl.BlockSpec((8, GROUP_W), lambda b, i: (0, 0))],
        out_specs=(spec(fwd, 0), spec(bwd, 0)),
        scratch_shapes=[pltpu.VMEM((2 * N_HEADS, HEAD_DIM, HEAD_DIM), F32)],
        compiler_params=_params("arbitrary", "arbitrary"), name="hgrn2_scan",
    )(p, p, p, p, p, p, lb)


def _gdn_kernel(qf_ref, kf_ref, vf_ref, sf_ref, qb_ref, kb_ref, vb_ref, sb_ref,
                ac_ref, of_ref, ob_ref, s_ref):
    @pl.when(pl.program_id(1) == 0)
    def _():
        s_ref[...] = jnp.zeros_like(s_ref)

    c = DN_CHUNK
    tb = DN_BLOCK
    levels = c.bit_length() - 1
    row = lax.broadcasted_iota(jnp.int32, (tb, tb), 0)
    colm = lax.broadcasted_iota(jnp.int32, (tb, tb), 1)
    same = (row >> levels) == (colm >> levels)
    eye = (row == colm).astype(F32)
    nchunk = ROWS // c

    chains, solves = [], []
    for d, refs in enumerate(((qf_ref, kf_ref, vf_ref, sf_ref), (qb_ref, kb_ref, vb_ref, sb_ref))):
        q_ref, k_ref, v_ref, sm_ref = refs
        reverse = d == 1
        sm = sm_ref[0]
        beta_c = _sigmoid(sm)
        la_c = -jnp.exp(ac_ref[0:1, :]) * _softplus(sm + ac_ref[1:2, :])
        cum_c, tot_c = _block_prefix(la_c, levels, reverse, 0)[levels]
        cum_r = cum_c.T
        incl = jnp.logical_and(same, (colm >= row) if reverse else (colm <= row))
        strict = jnp.logical_and(same, (colm > row) if reverse else (colm < row))
        q_all, k_all, v_all = q_ref[0], k_ref[0], v_ref[0]
        for h in range(N_HEADS):
            cs = slice(h * HEAD_DIM, (h + 1) * HEAD_DIM)
            bl = 4 * d + h
            al = 8 + 4 * d + h
            cc = cum_c[:, al:al + 1]
            tt = tot_c[:, al:al + 1]
            ch = dict(d=d, h=h, reverse=reverse, w=[], u=[], qk=[],
                      q_dec=(q_all[:, cs] * jnp.exp(cc)).astype(BF16),
                      k_dec=(k_all[:, cs] * jnp.exp(tt - cc)).astype(BF16),
                      g_tot=jnp.exp(tt), s=s_ref[d * N_HEADS + h], out=[None] * nchunk)
            chains.append(ch)
            for sb in range(ROWS // tb):
                rs = slice(sb * tb, (sb + 1) * tb)
                solves.append(dict(ch=ch, q=q_all[rs, cs], k=k_all[rs, cs], v=v_all[rs, cs], cc=cc[rs],
                                   cr=cum_r[al:al + 1, rs], bc=beta_c[rs, bl:bl + 1], incl=incl, strict=strict))
    for sv in solves:
        sv["decay"] = jnp.where(sv["incl"], jnp.exp(jnp.where(sv["incl"], sv["cc"] - sv["cr"], 0.0)), 0.0)
        sv["kb"] = sv["k"].astype(BF16)
    for sv in solves:
        sv["kk"] = _dot_nt(sv["kb"], sv["kb"])
    for sv in solves:
        sv["qk"] = _dot_nt(sv["q"].astype(BF16), sv["kb"])
    for sv in solves:
        sv["ch"]["qk"].append((sv["qk"] * sv["decay"]).astype(BF16))
        sv["pw"] = jnp.where(sv["strict"], sv["bc"] * sv["kk"] * sv["decay"], 0.0)
        sv["tinv"] = eye - sv["pw"]
    for _ in range(levels - 1):
        for sv in solves:
            pwb = sv["pw"].astype(BF16)
            sv["pw"] = _dot(pwb, pwb)
        for sv in solves:
            sv["tinv"] = sv["tinv"] + _dot(sv["tinv"].astype(BF16), sv["pw"].astype(BF16))
    for sv in solves:
        rhs = jnp.concatenate([sv["k"] * (sv["bc"] * jnp.exp(sv["cc"])), sv["v"] * sv["bc"]], axis=1)
        sv["wu"] = _dot(sv["tinv"].astype(BF16), rhs.astype(BF16))
    for sv in solves:
        sv["ch"]["w"].append(sv["wu"][:, :HEAD_DIM])
        sv["ch"]["u"].append(sv["wu"][:, HEAD_DIM:])

    per = tb // c
    for step in range(nchunk):
        for ch in chains:
            ci = nchunk - 1 - step if ch["reverse"] else step
            ch["ci"], ch["rs"] = ci, slice(ci * c, (ci + 1) * c)
            ch["sbi"] = ci // per
            ch["sub"] = slice((ci % per) * c, (ci % per) * c + c)
            wq = jnp.concatenate([ch["w"][ch["sbi"]][ch["sub"]].astype(BF16), ch["q_dec"][ch["rs"]]], axis=0)
            ch["ws"] = _dot(wq, ch["s"].astype(BF16))
        for ch in chains:
            ch["v_new"] = (ch["u"][ch["sbi"]][ch["sub"]] - ch["ws"][:c]).astype(BF16)
            ch["upd"] = _dot_tn(ch["k_dec"][ch["rs"]], ch["v_new"])
        for ch in chains:
            ci = ch["ci"]
            ch["out"][ci] = ch["ws"][c:] + _dot(ch["qk"][ch["sbi"]][ch["sub"], ch["sub"]], ch["v_new"])
            ch["s"] = ch["g_tot"][ci * c:ci * c + 1] * ch["s"] + ch["upd"]

    for d, o_ref in enumerate((of_ref, ob_ref)):
        mine = [ch for ch in chains if ch["d"] == d]
        o_ref[0] = jnp.concatenate([jnp.concatenate(ch["out"], axis=0) for ch in mine], axis=1)
        for ch in mine:
            s_ref[d * N_HEADS + ch["h"]] = ch["s"]


def _gdn(qkv, p, acoef_c):
    nbatch, lt, _ = qkv.shape
    nblk = lt // ROWS
    fwd, bwd = _scan_blocks(nblk)

    def spec(blk, cb):
        return pl.BlockSpec((1, ROWS, GROUP_W), lambda b, i: (b, blk(b, i), cb))

    def small(blk):
        return pl.BlockSpec((1, ROWS, LANES), lambda b, i: (b, blk(b, i), SMALL_BLK))

    oshape = jax.ShapeDtypeStruct((nbatch, lt, GROUP_W), F32)
    return pl.pallas_call(
        _gdn_kernel,
        out_shape=(oshape, oshape),
        grid=(nbatch, nblk),
        in_specs=[spec(fwd, 0), spec(fwd, 1), spec(fwd, 2), small(fwd),
                  spec(bwd, 0), spec(bwd, 1), spec(bwd, 2), small(bwd),
                  pl.BlockSpec((8, LANES), lambda b, i: (0, 0))],
        out_specs=(spec(fwd, 0), spec(bwd, 0)),
        scratch_shapes=[pltpu.VMEM((2 * N_HEADS, HEAD_DIM, HEAD_DIM), F32)],
        compiler_params=_params("arbitrary", "arbitrary"), name="gated_deltanet_scan",
    )(qkv, qkv, qkv, p, qkv, qkv, qkv, p, acoef_c)


def _outproj_kernel(x_ref, hf_ref, hb_ref, df_ref, db_ref, g_ref, z_ref, m_ref, hw_ref, dw_ref,
                    wo_ref, n2_ref, wr_ref, br_ref, xo_ref, h2_ref, meta_ref, cnt_ref, carry_ref):
    parts = []
    for (a_ref, b_ref, gate_ref, nw_ref) in ((hf_ref, hb_ref, g_ref, hw_ref), (df_ref, db_ref, z_ref, dw_ref)):
        o = a_ref[0] + b_ref[0]
        gate = gate_ref[0]
        for h in range(N_HEADS):
            cs = slice(h * HEAD_DIM, (h + 1) * HEAD_DIM)
            parts.append((_rms(o[:, cs], nw_ref[...]) * _silu(gate[:, cs])).astype(BF16))
    y = jnp.concatenate(parts, axis=1)
    x1 = x_ref[0] + m_ref[0, 2:3, :] * _dot(y, wo_ref[...])
    xo_ref[0] = x1
    h2 = _rms(x1, n2_ref[...]) * (1.0 + m_ref[0, 4:5, :]) + m_ref[0, 3:4, :]
    h2_ref[0, :, :D_MODEL] = h2
    h_hi = h2.astype(BF16)
    h_lo = (h2 - h_hi.astype(F32)).astype(BF16)
    wr = wr_ref[...]
    w_hi = wr.astype(BF16)
    w_lo = (wr - w_hi.astype(F32)).astype(BF16)
    logits = _dot(h_hi, w_hi) + (_dot(h_hi, w_lo) + _dot(h_lo, w_hi)) + br_ref[...]
    cls, gate_a, gate_b = _route(logits)

    @pl.when(jnp.logical_and(pl.program_id(0) == 0, pl.program_id(1) == 0))
    def _():
        carry_ref[...] = jnp.zeros_like(carry_ref)

    lane = lax.broadcasted_iota(jnp.int32, (ROWS, LANES), 1).astype(F32)
    onehot = jnp.where(lane == cls, 1.0, 0.0)
    r_i = lax.broadcasted_iota(jnp.int32, (ROWS, ROWS), 0)
    c_i = lax.broadcasted_iota(jnp.int32, (ROWS, ROWS), 1)
    earlier = jnp.where(c_i < r_i, 1.0, 0.0).astype(BF16)
    before = _dot(earlier, onehot.astype(BF16))
    carry = carry_ref[0:1, :]
    rank = jnp.sum(onehot * (before + carry), axis=-1, keepdims=True)
    carry = carry + jnp.sum(onehot, axis=0, keepdims=True)
    carry_ref[0:1, :] = carry
    cnt_ref[...] = jnp.broadcast_to(carry, cnt_ref.shape)
    route = (jnp.where(lane == float(ROUTE_CLS), cls, 0.0)
             + jnp.where(lane == float(ROUTE_GATE_A), gate_a, 0.0)
             + jnp.where(lane == float(ROUTE_GATE_B), gate_b, 0.0)
             + jnp.where(lane == float(ROUTE_RANK), rank, 0.0))
    h2_ref[0, :, D_MODEL:] = route
    meta_ref[0, 0] = route.T[0:8, :]


def _outproj(x, ohf, ohb, odf, odb, p, mod, hg_nw, dn_nw, w_out, norm2_w, w_route, b_route):
    nbatch, lt, d = x.shape
    grid = (nbatch, lt // ROWS)
    xspec = pl.BlockSpec((1, ROWS, d), lambda b, i: (b, i, 0))
    ospec = pl.BlockSpec((1, ROWS, GROUP_W), lambda b, i: (b, i, 0))

    def pcol(cb):
        return pl.BlockSpec((1, ROWS, GROUP_W), lambda b, i: (b, i, cb))

    def full(shape):
        return pl.BlockSpec(shape, lambda b, i: tuple(0 for _ in shape))

    return pl.pallas_call(
        _outproj_kernel,
        out_shape=(jax.ShapeDtypeStruct(x.shape, F32), jax.ShapeDtypeStruct((nbatch, lt, d + LANES), F32),
                   jax.ShapeDtypeStruct((nbatch, lt // ROWS, 8, ROWS), F32),
                   jax.ShapeDtypeStruct((8, LANES), F32)),
        grid=grid,
        in_specs=[xspec, ospec, ospec, ospec, ospec, pcol(HG_G), pcol(DN_Z),
                  pl.BlockSpec((1, 6, d), _mod_row(nbatch)),
                  full((1, HEAD_DIM)), full((1, HEAD_DIM)), full((d, d)), full((1, d)),
                  full((d, LANES)), full((1, LANES))],
        out_specs=(xspec, pl.BlockSpec((1, ROWS, d + LANES), lambda b, i: (b, i, 0)),
                   pl.BlockSpec((1, 1, 8, ROWS), lambda b, i: (b, i, 0, 0)), full((8, LANES))),
        scratch_shapes=[pltpu.VMEM((8, LANES), F32)],
        compiler_params=_params("arbitrary", "arbitrary"), name="out_proj",
    )(x, ohf, ohb, odf, odb, p, p, mod, hg_nw, dn_nw, w_out, norm2_w, w_route, b_route)


EXPERT_TILE = 256
MOVE_TILE = 512


def _plan(cls, rank, counts):
    t = cls.shape[0]
    tiles_c = (counts + EXPERT_TILE - 1) // EXPERT_TILE
    tile_end = jnp.cumsum(tiles_c)
    tile_start = tile_end - tiles_c
    pos = tile_start[cls] * EXPERT_TILE + rank
    ntiles = t // EXPERT_TILE + N_CLASSES
    tile_id = jnp.arange(ntiles, dtype=jnp.int32)
    tile_cls = jnp.minimum(jnp.sum((tile_id[:, None] >= tile_end[None, :]).astype(jnp.int32), axis=1),
                           N_CLASSES - 1)
    used = (tile_id < tile_end[-1]).astype(jnp.int32)
    pair_a = jnp.array([a for a, _ in PAIRS], jnp.int32)
    pair_b = jnp.array([b for _, b in PAIRS], jnp.int32)
    group = tile_cls // len(PAIRS)
    ea = group * EXPERTS_PER_GROUP + pair_a[tile_cls % len(PAIRS)]
    eb = group * EXPERTS_PER_GROUP + pair_b[tile_cls % len(PAIRS)]
    return pos, ntiles, ea, eb, used


def _scatter_kernel(idx_ref, x_ref, init_ref, dst_ref, sem):
    rows = x_ref.shape[0]

    def issue(r, carry):
        pltpu.make_async_copy(x_ref.at[pl.ds(r, 1), :], dst_ref.at[pl.ds(idx_ref[0, 0, r], 1), :], sem).start()
        return carry

    lax.fori_loop(0, rows, issue, 0, unroll=8)
    pltpu.make_async_copy(x_ref, dst_ref.at[pl.ds(0, rows), :], sem).wait()


def _gather_kernel(idx_ref, src_ref, o_ref, sem):
    rows = o_ref.shape[0]

    def issue(r, carry):
        pltpu.make_async_copy(src_ref.at[pl.ds(idx_ref[0, 0, r], 1), :], o_ref.at[pl.ds(r, 1), :], sem).start()
        return carry

    lax.fori_loop(0, rows, issue, 0, unroll=8)
    pltpu.make_async_copy(src_ref.at[pl.ds(0, rows), :], o_ref, sem).wait()


def _move_rows(src, idx, n_out, scatter):
    n = idx.shape[0]
    d = src.shape[1]
    tm = MOVE_TILE
    idx_spec = pl.BlockSpec((1, 1, tm), lambda i: (i, 0, 0), memory_space=pltpu.SMEM)
    hbm = pl.BlockSpec(memory_space=pl.ANY)
    blk = pl.BlockSpec((tm, d), lambda i: (i, 0))
    common = dict(out_shape=jax.ShapeDtypeStruct((n_out, d), src.dtype), grid=(n // tm,),
                  scratch_shapes=[pltpu.SemaphoreType.DMA], compiler_params=_params("arbitrary"))
    idx3 = idx.reshape(n // tm, 1, tm)
    if scatter:
        return pl.pallas_call(_scatter_kernel, in_specs=[idx_spec, blk, hbm], out_specs=hbm,
                              input_output_aliases={2: 0}, name="scatter_rows", **common
                              )(idx3, src, jnp.zeros((n_out, d), src.dtype))
    return pl.pallas_call(_gather_kernel, in_specs=[idx_spec, hbm], out_specs=blk,
                          name="gather_rows", **common)(idx3, src)


def _expert_kernel(ea_ref, eb_ref, used_ref, x_ref, wga_ref, wua_ref, wda_ref, wgb_ref, wub_ref, wdb_ref, o_ref):
    i = pl.program_id(0)

    @pl.when(used_ref[i] != 0)
    def _():
        x = x_ref[:, :D_MODEL].astype(BF16)
        gate_a = x_ref[:, D_MODEL + ROUTE_GATE_A:D_MODEL + ROUTE_GATE_A + 1]
        gate_b = x_ref[:, D_MODEL + ROUTE_GATE_B:D_MODEL + ROUTE_GATE_B + 1]
        act_a = _silu(_dot(x, wga_ref[0])) * _dot(x, wua_ref[0]) * gate_a
        act_b = _silu(_dot(x, wgb_ref[0])) * _dot(x, wub_ref[0]) * gate_b
        o_ref[...] = _dot(act_a.astype(BF16), wda_ref[0]) + _dot(act_b.astype(BF16), wdb_ref[0])

    @pl.when(used_ref[i] == 0)
    def _():
        o_ref[...] = jnp.zeros_like(o_ref)


def _experts(xs_sorted, ea, eb, used, w_gate, w_up, w_down):
    n, dx = xs_sorted.shape
    _, d, ff = w_gate.shape
    tm = EXPERT_TILE
    wa = lambda i, ea, eb, used: (ea[i], 0, 0)
    wb = lambda i, ea, eb, used: (eb[i], 0, 0)
    row = lambda i, ea, eb, used: (i, 0)
    return pl.pallas_call(
        _expert_kernel,
        out_shape=jax.ShapeDtypeStruct((n, d), F32),
        grid_spec=pltpu.PrefetchScalarGridSpec(
            num_scalar_prefetch=3,
            grid=(n // tm,),
            in_specs=[pl.BlockSpec((tm, dx), row),
                      pl.BlockSpec((1, d, ff), wa), pl.BlockSpec((1, d, ff), wa), pl.BlockSpec((1, ff, d), wa),
                      pl.BlockSpec((1, d, ff), wb), pl.BlockSpec((1, d, ff), wb), pl.BlockSpec((1, ff, d), wb)],
            out_specs=pl.BlockSpec((tm, d), row)),
        compiler_params=_params("arbitrary"), name="moe_experts",
    )(ea, eb, used, xs_sorted, w_gate, w_up, w_down, w_gate, w_up, w_down)


def _moe(h2ext, meta, counts, w_gate, w_up, w_down):
    cls = meta[:, :, ROUTE_CLS, :].reshape(-1).astype(jnp.int32)
    rank = meta[:, :, ROUTE_RANK, :].reshape(-1).astype(jnp.int32)
    pos, ntiles, ea, eb, used = _plan(cls, rank, counts)
    xs_sorted = _move_rows(h2ext, pos, ntiles * EXPERT_TILE, scatter=True)
    ys_sorted = _experts(xs_sorted, ea, eb, used, w_gate, w_up, w_down)
    return _move_rows(ys_sorted, pos, h2ext.shape[0], scatter=False)


def _route(lg):
    lane = lax.broadcasted_iota(jnp.int32, lg.shape, 1).astype(F32)
    neg = -jnp.inf
    big = float(LANES)

    def first_argmax(vals):
        m = jnp.max(vals, axis=-1, keepdims=True)
        return m, jnp.min(jnp.where(vals == m, lane, big), axis=-1, keepdims=True)

    in_groups = lane < N_GROUPS
    gl = jnp.where(in_groups, lg, neg)
    gmax, gidx = first_argmax(gl)
    g_w = 1.0 / jnp.sum(jnp.where(in_groups, jnp.exp(gl - gmax), 0.0), axis=-1, keepdims=True)
    lo = ROUTE_OFF + EXPERTS_PER_GROUP * gidx
    el = jnp.where(jnp.logical_and(lane >= lo, lane < lo + EXPERTS_PER_GROUP), lg, neg)
    m1, i1 = first_argmax(el)
    el2 = jnp.where(lane == i1, neg, el)
    m2, i2 = first_argmax(el2)
    e2 = jnp.exp(m2 - m1)
    w1 = g_w / (1.0 + e2)
    w2 = g_w * e2 / (1.0 + e2)
    first_lower = i1 < i2
    a = jnp.minimum(i1, i2) - lo
    b = jnp.maximum(i1, i2) - lo
    pair = a * (EXPERTS_PER_GROUP - 1) - a * (a - 1.0) * 0.5 + (b - a - 1.0)
    cls = gidx * float(len(PAIRS)) + pair
    return cls, jnp.where(first_lower, w1, w2), jnp.where(first_lower, w2, w1)


def _final_kernel(x_ref, y_ref, m_ref, nw_ref, o_ref):
    x = x_ref[0] + m_ref[0, 5:6, :] * y_ref[0]
    o_ref[0] = _rms(x, nw_ref[...])


def _final(x, y, mod, norm_w, ctx_blocks):
    nbatch, lt, d = x.shape
    nlat = lt // ROWS - ctx_blocks
    xspec = pl.BlockSpec((1, ROWS, d), lambda b, i: (b, i + ctx_blocks, 0))
    return pl.pallas_call(
        _final_kernel,
        out_shape=jax.ShapeDtypeStruct((nbatch, nlat * ROWS, d), F32),
        grid=(nbatch, nlat),
        in_specs=[xspec, xspec, pl.BlockSpec((1, 6, d), lambda b, i: (b, 0, 0)),
                  pl.BlockSpec((1, d), lambda b, i: (0, 0))],
        out_specs=pl.BlockSpec((1, ROWS, d), lambda b, i: (b, i, 0)),
        compiler_params=_params("arbitrary", "arbitrary"), name="final_norm",
    )(x, y, mod, norm_w)


def kernel(x, c, ctx, c_ctx, w_mod, b_mod, norm1_w, norm2_w, w_in, w_out, hg_lb, hg_norm_w, dn_conv_w,
           dn_a_log, dn_dt_bias, dn_norm_w, w_group, b_group, w_expert, b_expert, w_gate, w_up, w_down,
           final_norm_w):
    nbatch, seq, d = x.shape
    ctx_len = ctx.shape[1]
    depth = w_mod.shape[0]
    assert d == D_MODEL and ctx_len == ROWS and seq % ROWS == 0 and ROWS % GRID_W == 0
    assert w_in.shape[2] == IN_COLS
    lt = ctx_len + seq

    nb = -(-(nbatch + 1) // 8) * 8
    cvec = jnp.concatenate([c, c_ctx[None, :], jnp.zeros((nb - nbatch - 1, d), F32)], axis=0)
    mods = _modulation(cvec, w_mod, b_mod).reshape(depth, nb, 6, d)
    lbs = _lower_bounds(hg_lb)

    c_dn = 5 * GROUP_W
    w_qkv = w_in[:, :, c_dn:c_dn + 3 * GROUP_W].astype(BF16)
    w_rest = jnp.concatenate([w_in[:, :, :c_dn], w_in[:, :, c_dn + 3 * GROUP_W:],
                              jnp.zeros((depth, d, REST_COLS + 3 * GROUP_W - IN_COLS), F32)], axis=2).astype(BF16)
    w_out_b = w_out.astype(BF16)
    w_gate_b, w_up_b, w_down_b = w_gate.astype(BF16), w_up.astype(BF16), w_down.astype(BF16)
    nroute = N_GROUPS + N_EXPERTS
    w_route = jnp.pad(jnp.concatenate([w_group, w_expert], axis=2), ((0, 0), (0, 0), (0, LANES - nroute)))
    b_route = jnp.pad(jnp.concatenate([b_group, b_expert], axis=1), ((0, 0), (0, LANES - nroute)))
    conv_w = dn_conv_w.reshape(depth, 9, 3 * GROUP_W)
    a_flat = dn_a_log.reshape(depth, 2 * N_HEADS)
    dt_flat = dn_dt_bias.reshape(depth, 2 * N_HEADS)
    lane_vals = lambda v: jnp.pad(v, ((0, 0), (2 * N_HEADS, LANES - 4 * N_HEADS)))
    acoef_c = jnp.pad(jnp.stack([lane_vals(a_flat), lane_vals(dt_flat)], axis=1), ((0, 0), (0, 6), (0, 0)))

    xs = jnp.concatenate([ctx, x], axis=1)
    y = None
    for l in range(depth):
        xs, qkv, p = _inproj(xs, y, mods[l - 1] if l else None, mods[l], norm1_w[l][None, :],
                             w_qkv[l], w_rest[l], conv_w[l])
        ohf, ohb = _hgrn(p, lbs[l])
        odf, odb = _gdn(qkv, p, acoef_c[l])
        xs, h2ext, meta, counts = _outproj(xs, ohf, ohb, odf, odb, p, mods[l], hg_norm_w[l][None, :],
                                           dn_norm_w[l][None, :], w_out_b[l], norm2_w[l][None, :],
                                           w_route[l], b_route[l][None, :])
        y = _moe(h2ext.reshape(nbatch * lt, d + LANES), meta, counts[0, :N_CLASSES].astype(jnp.int32),
                 w_gate_b[l], w_up_b[l], w_down_b[l]).reshape(nbatch, lt, d)
    return _final(xs, y, mods[depth - 1], final_norm_w[None, :], ctx_len // ROWS)
```

```python
import functools

import jax
import jax.numpy as jnp
from jax import lax
from jax.experimental import pallas as pl
from jax.experimental.pallas import tpu as pltpu

F32 = jnp.float32
BF16 = jnp.bfloat16

D_MODEL = 1024
N_HEADS = 4
HEAD_DIM = 128
GROUP_W = N_HEADS * HEAD_DIM
GRID_W = 64
N_GROUPS = 4
EXPERTS_PER_GROUP = 4
N_EXPERTS = N_GROUPS * EXPERTS_PER_GROUP
EXPERT_FF = 512
NORM_EPS = 1e-6
LANES = 128
IN_COLS = 9 * GROUP_W + 4 * N_HEADS
REST_COLS = 6 * GROUP_W + LANES
SMALL_BLK = 6 * GROUP_W // LANES
ROWS = 256
TILE = 128
DN_CHUNK = 64
DN_BLOCK = 128
ROUTE_OFF = N_GROUPS
PAIRS = [(a, b) for a in range(EXPERTS_PER_GROUP) for b in range(a + 1, EXPERTS_PER_GROUP)]
N_CLASSES = N_GROUPS * len(PAIRS)
ROUTE_CLS, ROUTE_GATE_A, ROUTE_GATE_B, ROUTE_RANK = 0, 1, 2, 3
VMEM_LIMIT = 56 * 1024 * 1024

HG_Q, HG_I, HG_G, HG_FF, HG_FB, DN_Z = range(6)


def _sigmoid(x):
    return 1.0 / (1.0 + jnp.exp(-x))


def _silu(x):
    return x * _sigmoid(x)


def _softplus(x):
    return jnp.maximum(x, 0.0) + jnp.log(1.0 + jnp.exp(-jnp.abs(x)))


def _dot(a, b):
    return jnp.dot(a, b, preferred_element_type=F32)


def _dot_nt(a, b):
    return lax.dot_general(a, b, (((1,), (1,)), ((), ())), preferred_element_type=F32)


def _dot_tn(a, b):
    return lax.dot_general(a, b, (((0,), (0,)), ((), ())), preferred_element_type=F32)


def _rms(x, w):
    return x * lax.rsqrt(jnp.mean(x * x, axis=-1, keepdims=True) + NORM_EPS) * w


def _params(*sem):
    return pltpu.CompilerParams(dimension_semantics=sem, vmem_limit_bytes=VMEM_LIMIT)


def _mod_kernel(c_ref, w_ref, b_ref, o_ref):
    a = _silu(c_ref[...])
    o_ref[0] = jnp.dot(a, w_ref[0], preferred_element_type=F32,
                       precision=lax.Precision.HIGHEST) + b_ref[0]


def _modulation(cvec, w_mod, b_mod):
    depth, d, n = w_mod.shape
    nb = cvec.shape[0]
    tn = 1536
    return pl.pallas_call(
        _mod_kernel,
        out_shape=jax.ShapeDtypeStruct((depth, nb, n), F32),
        grid=(depth, n // tn),
        in_specs=[pl.BlockSpec((nb, d), lambda l, j: (0, 0)),
                  pl.BlockSpec((1, d, tn), lambda l, j: (l, 0, j)),
                  pl.BlockSpec((1, 1, tn), lambda l, j: (l, 0, j))],
        out_specs=pl.BlockSpec((1, nb, tn), lambda l, j: (l, 0, j)),
        compiler_params=_params("arbitrary", "arbitrary"),
        name="adaln_modulation",
    )(cvec, w_mod, b_mod.reshape(depth, 1, n))


def _lb_kernel(depth, lb_ref, o_ref):
    x = [lb_ref[l] for l in range(depth)]
    m = functools.reduce(jnp.maximum, x)
    e = [jnp.exp(v - m) for v in x]
    tot = functools.reduce(lambda a, b: a + b, e)
    sm = [v / tot for v in e]
    cum = sm[0]
    first = cum
    for l in range(depth):
        if l > 0:
            cum = cum + sm[l]
        lb = cum - first
        o_ref[l, 0:2, :] = jnp.log(lb)
        o_ref[l, 2:4, :] = jnp.log1p(-lb)
        o_ref[l, 4:6, :] = 1.0 - lb
        o_ref[l, 6:8, :] = jnp.zeros_like(lb)


def _lower_bounds(hg_lb):
    depth = hg_lb.shape[0]
    return pl.pallas_call(
        functools.partial(_lb_kernel, depth),
        out_shape=jax.ShapeDtypeStruct((depth, 8, GROUP_W), F32),
        name="hgrn_lower_bounds",
    )(hg_lb)


def _short_conv(xe, w_ref, cs, is_ctx):
    n_ext = xe.shape[0]
    col = lax.broadcasted_iota(jnp.int32, (n_ext, 1), 0) % GRID_W
    acc = None
    for dw in (-1, 0, 1):
        if dw == 0:
            xs = xe
        else:
            xs = pltpu.roll(xe, (-dw) % n_ext, 0)
            inside = jnp.logical_and(col + dw >= 0, col + dw < GRID_W)
            xs = jnp.where(jnp.logical_or(is_ctx, inside), xs, 0.0)
        for dh in (-1, 0, 1):
            tap = (dh + 1) * 3 + (dw + 1)
            w = w_ref[tap:tap + 1, cs]
            if dh != 0:
                w = jnp.where(is_ctx, 0.0, w)
            term = xs[GRID_W * (1 + dh):GRID_W * (1 + dh) + ROWS, :] * w
            acc = term if acc is None else acc + term
    return acc


def _inproj_kernel(has_y, nblk, *refs):
    if has_y:
        (x_ref, xp_ref, xn_ref, y_ref, yp_ref, yn_ref, pm_ref, m_ref, nw_ref, wq_ref, wr_ref, cw_ref,
         xo_ref, qkv_ref, p_ref) = refs
        g2 = pm_ref[0, 5:6, :]
        x = x_ref[0] + g2 * y_ref[0]
        xo_ref[0] = x
        x_prev = xp_ref[0] + g2 * yp_ref[0]
        x_next = xn_ref[0] + g2 * yn_ref[0]
    else:
        x_ref, xp_ref, xn_ref, m_ref, nw_ref, wq_ref, wr_ref, cw_ref, qkv_ref, p_ref = refs
        x, x_prev, x_next = x_ref[0], xp_ref[0], xn_ref[0]
    i = pl.program_id(1)
    is_ctx = i == 0
    prev_ok = i >= 2
    next_ok = jnp.logical_and(i >= 1, i < nblk - 1)
    xe = jnp.concatenate([x_prev, x, x_next], axis=0)
    he = (_rms(xe, nw_ref[...]) * (1.0 + m_ref[0, 1:2, :]) + m_ref[0, 0:1, :]).astype(BF16)
    row = lax.broadcasted_iota(jnp.int32, (ROWS + 2 * GRID_W, 1), 0)
    halo_ok = jnp.logical_and(jnp.logical_or(row >= GRID_W, prev_ok),
                              jnp.logical_or(row < GRID_W + ROWS, next_ok))
    nrest = p_ref.shape[2]
    pair_w = 2 * HEAD_DIM
    npair = 3 * GROUP_W // pair_w
    rest_step = nrest // npair // pair_w * pair_w
    hm = he[GRID_W:GRID_W + ROWS]
    done = 0
    for pi in range(npair):
        pq = jnp.where(halo_ok, _dot(he, wq_ref[:, pi * pair_w:(pi + 1) * pair_w]), 0.0)
        upto = nrest if pi == npair - 1 else done + rest_step
        p_ref[0, :, done:upto] = _dot(hm, wr_ref[:, done:upto])
        done = upto
        sec = pi * pair_w // GROUP_W
        for hh in range(2):
            cs = slice(pi * pair_w + hh * HEAD_DIM, pi * pair_w + (hh + 1) * HEAD_DIM)
            yh = _silu(_short_conv(pq[:, hh * HEAD_DIM:(hh + 1) * HEAD_DIM], cw_ref, cs, is_ctx))
            if sec < 2:
                yh = yh * lax.rsqrt(jnp.sum(yh * yh, axis=-1, keepdims=True) + NORM_EPS)
                if sec == 0:
                    yh = yh * HEAD_DIM ** -0.5
            qkv_ref[0, :, cs] = yh


def _mod_row(nbatch):
    return lambda b, i: (jnp.where(i == 0, nbatch, b), 0, 0)


def _inproj(x, y, prev_mod, mod, norm_w, w_qkv, w_rest, conv_w):
    nbatch, lt, d = x.shape
    nblk = lt // ROWS
    hpb = ROWS // GRID_W
    nhalo = lt // GRID_W
    nrest = w_rest.shape[1]
    nqkv = w_qkv.shape[1]
    grid = (nbatch, nblk)
    xspec = pl.BlockSpec((1, ROWS, d), lambda b, i: (b, i, 0))
    prev = pl.BlockSpec((1, GRID_W, d), lambda b, i: (b, jnp.maximum(i * hpb - 1, 0), 0))
    nxt = pl.BlockSpec((1, GRID_W, d), lambda b, i: (b, jnp.minimum((i + 1) * hpb, nhalo - 1), 0))
    mspec = pl.BlockSpec((1, 6, d), _mod_row(nbatch))

    def full(shape):
        return pl.BlockSpec(shape, lambda b, i: tuple(0 for _ in shape))

    wspecs = [full((1, d)), full((d, nqkv)), full((d, nrest)), full((9, nqkv))]
    outs = (jax.ShapeDtypeStruct((nbatch, lt, nqkv), F32), jax.ShapeDtypeStruct((nbatch, lt, nrest), F32))
    ospecs = (pl.BlockSpec((1, ROWS, nqkv), lambda b, i: (b, i, 0)),
              pl.BlockSpec((1, ROWS, nrest), lambda b, i: (b, i, 0)))
    if y is None:
        qkv, p = pl.pallas_call(
            functools.partial(_inproj_kernel, False, nblk),
            out_shape=outs, grid=grid,
            in_specs=[xspec, prev, nxt, mspec] + wspecs, out_specs=ospecs,
            compiler_params=_params("arbitrary", "arbitrary"), name="in_proj",
        )(x, x, x, mod, norm_w, w_qkv, w_rest, conv_w)
        return x, qkv, p
    return pl.pallas_call(
        functools.partial(_inproj_kernel, True, nblk),
        out_shape=(jax.ShapeDtypeStruct(x.shape, F32),) + outs, grid=grid,
        in_specs=[xspec, prev, nxt, xspec, prev, nxt, mspec, mspec] + wspecs, out_specs=(xspec,) + ospecs,
        compiler_params=_params("arbitrary", "arbitrary"), name="in_proj_res",
    )(x, x, x, y, y, y, prev_mod, mod, norm_w, w_qkv, w_rest, conv_w)


def _block_prefix(g, levels, reverse, axis):
    n = g.shape[axis]
    idx = lax.broadcasted_iota(jnp.int32, g.shape, axis)
    pre, tot = g, g
    out = []
    for j in range(levels):
        out.append((pre, tot))
        half = 1 << j
        upper = (idx & half) != 0
        below = pltpu.roll(tot, half, axis)
        above = pltpu.roll(tot, n - half, axis)
        if reverse:
            pre = pre + jnp.where(upper, 0.0, above)
        else:
            pre = pre + jnp.where(upper, below, 0.0)
        tot = tot + jnp.where(upper, below, above)
    out.append((pre, tot))
    return out


def _pair_level_exponents(g, levels, reverse):
    rows, width = g.shape
    sub = 8
    idx = lax.broadcasted_iota(jnp.int32, g.shape, 0)
    pre, tot = g, g
    args = []
    for j in range(levels):
        half = 1 << j
        if half < sub:
            upper = (idx & half) != 0
            q_side = jnp.logical_not(upper) if reverse else upper
            args.append(jnp.where(q_side, pre, tot - pre))
            below = pltpu.roll(tot, half, 0)
            above = pltpu.roll(tot, rows - half, 0)
            if reverse:
                pre = pre + jnp.where(upper, 0.0, above)
            else:
                pre = pre + jnp.where(upper, below, 0.0)
            tot = tot + jnp.where(upper, below, above)
        else:
            nb = rows // (2 * half)
            p4 = pre.reshape(nb, 2, half, width)
            t4 = tot.reshape(nb, 2, half, width)
            p_lo, p_hi, t_lo, t_hi = p4[:, 0], p4[:, 1], t4[:, 0], t4[:, 1]
            both = t_lo + t_hi
            if reverse:
                arg = jnp.stack([p_lo, t_hi - p_hi], axis=1)
                pre = jnp.stack([p_lo + t_hi, p_hi], axis=1)
            else:
                arg = jnp.stack([t_lo - p_lo, p_hi], axis=1)
                pre = jnp.stack([p_lo, p_hi + t_lo], axis=1)
            args.append(arg.reshape(rows, width))
            pre = pre.reshape(rows, width)
            tot = jnp.stack([both, both], axis=1).reshape(rows, width)
    return args, pre, tot


def _scan_blocks(nblk):
    fwd = lambda b, i: i
    bwd = lambda b, i: jnp.where(i == 0, 0, nblk - i)
    return fwd, bwd


HG_LEVELS = TILE.bit_length() - 1


def _hgrn_gates(qf_ref, vf_ref, ff_ref, qb_ref, vb_ref, fb_ref, lb_ref):
    levels = HG_LEVELS
    parts = []
    for d, (q_ref, v_ref, f_ref) in enumerate(((qf_ref, vf_ref, ff_ref), (qb_ref, vb_ref, fb_ref))):
        reverse = d == 1
        q = _silu(q_ref[0])
        v = v_ref[0]
        x = f_ref[0]
        log_lb = lb_ref[d:d + 1, :]
        log_1mlb = lb_ref[2 + d:3 + d, :]
        one_mlb = lb_ref[4 + d:5 + d, :]
        u = jnp.exp(-jnp.abs(x))
        b_term = log_1mlb + (jnp.minimum(x, 0.0) - jnp.log(1.0 + u))
        mx = jnp.maximum(log_lb, b_term)
        g = mx + jnp.log(1.0 + jnp.exp(-jnp.abs(log_lb - b_term)))
        k = one_mlb * (jnp.where(x >= 0.0, u, 1.0) / (1.0 + u))
        args, pre, tot = _pair_level_exponents(g, levels, reverse)
        q_in = (q * jnp.exp(pre)).astype(BF16)
        k_out = (k * jnp.exp(tot - pre)).astype(BF16)
        g_tile = jnp.exp(tot)
        qb16 = q.astype(BF16)
        kb16 = k.astype(BF16)
        vb16 = v.astype(BF16)
        qe, ke = [], []
        for j in range(levels):
            e = jnp.exp(args[j]).astype(BF16)
            qe.append(qb16 * e)
            ke.append(kb16 * e)
        parts.append((q_in, k_out, g_tile, qb16, kb16, vb16, qe, ke))
    return parts


def _hgrn_attend(parts, of_ref, ob_ref, st_ref):
    levels = HG_LEVELS
    row = lax.broadcasted_iota(jnp.int32, (TILE, TILE), 0)
    colm = lax.broadcasted_iota(jnp.int32, (TILE, TILE), 1)
    eye = row == colm
    lvl_masks = []
    for j in range(levels):
        same = (row >> (j + 1)) == (colm >> (j + 1))
        t_up = (row & (1 << j)) != 0
        s_up = (colm & (1 << j)) != 0
        fwd_m = jnp.logical_and(same, jnp.logical_and(t_up, jnp.logical_not(s_up)))
        bwd_m = jnp.logical_and(same, jnp.logical_and(jnp.logical_not(t_up), s_up))
        lvl_masks.append((fwd_m, bwd_m))
    ntile = ROWS // TILE
    for d, o_ref in enumerate((of_ref, ob_ref)):
        reverse = d == 1
        q_in, k_out, g_tile, qb16, kb16, vb16, qe, ke = parts[d]
        for tix in (range(ntile - 1, -1, -1) if reverse else range(ntile)):
            rs = slice(tix * TILE, (tix + 1) * TILE)
            for h in range(N_HEADS):
                cs = slice(h * HEAD_DIM, (h + 1) * HEAD_DIM)
                attn = jnp.where(eye, _dot_nt(qb16[rs, cs], kb16[rs, cs]), 0.0)
                for j in range(levels):
                    attn = jnp.where(lvl_masks[j][d], _dot_nt(qe[j][rs, cs], ke[j][rs, cs]), attn)
                st = st_ref[d * N_HEADS + h]
                o = _dot(attn.astype(BF16), vb16[rs, cs]) + _dot_nt(q_in[rs, cs], st.astype(BF16))
                o_ref[0, rs, cs] = o
                upd = _dot_tn(vb16[rs, cs], k_out[rs, cs])
                st_ref[d * N_HEADS + h] = st * g_tile[tix * TILE:tix * TILE + 1, cs] + upd


def _gdn_prepare(qf_ref, kf_ref, vf_ref, sf_ref, qb_ref, kb_ref, vb_ref, sb_ref, ac_ref, s_ref):
    c = DN_CHUNK
    tb = DN_BLOCK
    levels = c.bit_length() - 1
    row = lax.broadcasted_iota(jnp.int32, (tb, tb), 0)
    colm = lax.broadcasted_iota(jnp.int32, (tb, tb), 1)
    same = (row >> levels) == (colm >> levels)
    eye = (row == colm).astype(F32)
    nchunk = ROWS // c

    chains, solves = [], []
    for d, refs in enumerate(((qf_ref, kf_ref, vf_ref, sf_ref), (qb_ref, kb_ref, vb_ref, sb_ref))):
        q_ref, k_ref, v_ref, sm_ref = refs
        reverse = d == 1
        sm = sm_ref[0]
        beta_c = _sigmoid(sm)
        la_c = -jnp.exp(ac_ref[0:1, :]) * _softplus(sm + ac_ref[1:2, :])
        cum_c, tot_c = _block_prefix(la_c, levels, reverse, 0)[levels]
        cum_r = cum_c.T
        incl = jnp.logical_and(same, (colm >= row) if reverse else (colm <= row))
        strict = jnp.logical_and(same, (colm > row) if reverse else (colm < row))
        q_all, k_all, v_all = q_ref[0], k_ref[0], v_ref[0]
        for h in range(N_HEADS):
            cs = slice(h * HEAD_DIM, (h + 1) * HEAD_DIM)
            bl = 4 * d + h
            al = 8 + 4 * d + h
            cc = cum_c[:, al:al + 1]
            tt = tot_c[:, al:al + 1]
            ch = dict(d=d, h=h, reverse=reverse, w=[], u=[], qk=[],
                      q_dec=(q_all[:, cs] * jnp.exp(cc)).astype(BF16),
                      k_dec=(k_all[:, cs] * jnp.exp(tt - cc)).astype(BF16),
                      g_tot=jnp.exp(tt), s=s_ref[d * N_HEADS + h], out=[None] * nchunk)
            chains.append(ch)
            for sb in range(ROWS // tb):
                rs = slice(sb * tb, (sb + 1) * tb)
                solves.append(dict(ch=ch, q=q_all[rs, cs], k=k_all[rs, cs], v=v_all[rs, cs], cc=cc[rs],
                                   cr=cum_r[al:al + 1, rs], bc=beta_c[rs, bl:bl + 1], incl=incl, strict=strict))
    for sv in solves:
        sv["decay"] = jnp.where(sv["incl"], jnp.exp(jnp.where(sv["incl"], sv["cc"] - sv["cr"], 0.0)), 0.0)
        sv["kb"] = sv["k"].astype(BF16)
    for sv in solves:
        sv["kk"] = _dot_nt(sv["kb"], sv["kb"])
    for sv in solves:
        sv["qk"] = _dot_nt(sv["q"].astype(BF16), sv["kb"])
    for sv in solves:
        sv["ch"]["qk"].append((sv["qk"] * sv["decay"]).astype(BF16))
        sv["pw"] = jnp.where(sv["strict"], sv["bc"] * sv["kk"] * sv["decay"], 0.0)
        sv["tinv"] = eye - sv["pw"]
    for _ in range(levels - 1):
        for sv in solves:
            pwb = sv["pw"].astype(BF16)
            sv["pw"] = _dot(pwb, pwb)
        for sv in solves:
            sv["tinv"] = sv["tinv"] + _dot(sv["tinv"].astype(BF16), sv["pw"].astype(BF16))
    for sv in solves:
        rhs = jnp.concatenate([sv["k"] * (sv["bc"] * jnp.exp(sv["cc"])), sv["v"] * sv["bc"]], axis=1)
        sv["wu"] = _dot(sv["tinv"].astype(BF16), rhs.astype(BF16))
    for sv in solves:
        sv["ch"]["w"].append(sv["wu"][:, :HEAD_DIM])
        sv["ch"]["u"].append(sv["wu"][:, HEAD_DIM:])
    return chains


def _gdn_recur(chains, of_ref, ob_ref, s_ref):
    c = DN_CHUNK
    nchunk = ROWS // c
    per = DN_BLOCK // c
    for step in range(nchunk):
        for ch in chains:
            ci = nchunk - 1 - step if ch["reverse"] else step
            ch["ci"], ch["rs"] = ci, slice(ci * c, (ci + 1) * c)
            ch["sbi"] = ci // per
            ch["sub"] = slice((ci % per) * c, (ci % per) * c + c)
            wq = jnp.concatenate([ch["w"][ch["sbi"]][ch["sub"]].astype(BF16), ch["q_dec"][ch["rs"]]], axis=0)
            ch["ws"] = _dot(wq, ch["s"].astype(BF16))
        for ch in chains:
            ch["v_new"] = (ch["u"][ch["sbi"]][ch["sub"]] - ch["ws"][:c]).astype(BF16)
            ch["upd"] = _dot_tn(ch["k_dec"][ch["rs"]], ch["v_new"])
        for ch in chains:
            ci = ch["ci"]
            ch["out"][ci] = ch["ws"][c:] + _dot(ch["qk"][ch["sbi"]][ch["sub"], ch["sub"]], ch["v_new"])
            ch["s"] = ch["g_tot"][ci * c:ci * c + 1] * ch["s"] + ch["upd"]

    for d, o_ref in enumerate((of_ref, ob_ref)):
        mine = [ch for ch in chains if ch["d"] == d]
        o_ref[0] = jnp.concatenate([jnp.concatenate(ch["out"], axis=0) for ch in mine], axis=1)
        for ch in mine:
            s_ref[d * N_HEADS + ch["h"]] = ch["s"]


def _scan_kernel(hqf, hvf, hff, hqb, hvb, hfb, lb_ref, dqf, dkf, dvf, dsf, dqb, dkb, dvb, dsb, ac_ref,
                 ohf_ref, ohb_ref, odf_ref, odb_ref, st_ref, s_ref):
    @pl.when(pl.program_id(1) == 0)
    def _():
        st_ref[...] = jnp.zeros_like(st_ref)
        s_ref[...] = jnp.zeros_like(s_ref)

    parts = _hgrn_gates(hqf, hvf, hff, hqb, hvb, hfb, lb_ref)
    chains = _gdn_prepare(dqf, dkf, dvf, dsf, dqb, dkb, dvb, dsb, ac_ref, s_ref)
    _hgrn_attend(parts, ohf_ref, ohb_ref, st_ref)
    _gdn_recur(chains, odf_ref, odb_ref, s_ref)


def _scans(p, qkv, lb, acoef_c):
    nbatch, lt, _ = p.shape
    nblk = lt // ROWS
    fwd, bwd = _scan_blocks(nblk)

    def spec(blk, cb):
        return pl.BlockSpec((1, ROWS, GROUP_W), lambda b, i: (b, blk(b, i), cb))

    def small(blk):
        return pl.BlockSpec((1, ROWS, LANES), lambda b, i: (b, blk(b, i), SMALL_BLK))

    oshape = jax.ShapeDtypeStruct((nbatch, lt, GROUP_W), F32)
    state = pltpu.VMEM((2 * N_HEADS, HEAD_DIM, HEAD_DIM), F32)
    return pl.pallas_call(
        _scan_kernel,
        out_shape=(oshape, oshape, oshape, oshape),
        grid=(nbatch, nblk),
        in_specs=[spec(fwd, HG_Q), spec(fwd, HG_I), spec(fwd, HG_FF),
                  spec(bwd, HG_Q), spec(bwd, HG_I), spec(bwd, HG_FB),
                  pl.BlockSpec((8, GROUP_W), lambda b, i: (0, 0)),
                  spec(fwd, 0), spec(fwd, 1), spec(fwd, 2), small(fwd),
                  spec(bwd, 0), spec(bwd, 1), spec(bwd, 2), small(bwd),
                  pl.BlockSpec((8, LANES), lambda b, i: (0, 0))],
        out_specs=(spec(fwd, 0), spec(bwd, 0), spec(fwd, 0), spec(bwd, 0)),
        scratch_shapes=[state, state],
        compiler_params=_params("arbitrary", "arbitrary"), name="bidirectional_scans",
    )(p, p, p, p, p, p, lb, qkv, qkv, qkv, p, qkv, qkv, qkv, p, acoef_c)


def _outproj_kernel(x_ref, hf_ref, hb_ref, df_ref, db_ref, g_ref, z_ref, m_ref, hw_ref, dw_ref,
                    wo_ref, n2_ref, wr_ref, br_ref, xo_ref, h2_ref, meta_ref, cnt_ref, carry_ref):
    parts = []
    for (a_ref, b_ref, gate_ref, nw_ref) in ((hf_ref, hb_ref, g_ref, hw_ref), (df_ref, db_ref, z_ref, dw_ref)):
        o = a_ref[0] + b_ref[0]
        gate = gate_ref[0]
        for h in range(N_HEADS):
            cs = slice(h * HEAD_DIM, (h + 1) * HEAD_DIM)
            parts.append((_rms(o[:, cs], nw_ref[...]) * _silu(gate[:, cs])).astype(BF16))
    y = jnp.concatenate(parts, axis=1)
    x1 = x_ref[0] + m_ref[0, 2:3, :] * _dot(y, wo_ref[...])
    xo_ref[0] = x1
    h2 = _rms(x1, n2_ref[...]) * (1.0 + m_ref[0, 4:5, :]) + m_ref[0, 3:4, :]
    h2_ref[0, :, :D_MODEL] = h2
    h_hi = h2.astype(BF16)
    h_lo = (h2 - h_hi.astype(F32)).astype(BF16)
    wr = wr_ref[...]
    w_hi = wr.astype(BF16)
    w_lo = (wr - w_hi.astype(F32)).astype(BF16)
    logits = _dot(h_hi, w_hi) + (_dot(h_hi, w_lo) + _dot(h_lo, w_hi)) + br_ref[...]
    cls, gate_a, gate_b = _route(logits)

    @pl.when(jnp.logical_and(pl.program_id(0) == 0, pl.program_id(1) == 0))
    def _():
        carry_ref[...] = jnp.zeros_like(carry_ref)

    lane = lax.broadcasted_iota(jnp.int32, (ROWS, LANES), 1).astype(F32)
    onehot = jnp.where(lane == cls, 1.0, 0.0)
    r_i = lax.broadcasted_iota(jnp.int32, (ROWS, ROWS), 0)
    c_i = lax.broadcasted_iota(jnp.int32, (ROWS, ROWS), 1)
    earlier = jnp.where(c_i < r_i, 1.0, 0.0).astype(BF16)
    before = _dot(earlier, onehot.astype(BF16))
    carry = carry_ref[0:1, :]
    rank = jnp.sum(onehot * (before + carry), axis=-1, keepdims=True)
    carry = carry + jnp.sum(onehot, axis=0, keepdims=True)
    carry_ref[0:1, :] = carry
    cnt_ref[...] = jnp.broadcast_to(carry, cnt_ref.shape)
    route = (jnp.where(lane == float(ROUTE_CLS), cls, 0.0)
             + jnp.where(lane == float(ROUTE_GATE_A), gate_a, 0.0)
             + jnp.where(lane == float(ROUTE_GATE_B), gate_b, 0.0)
             + jnp.where(lane == float(ROUTE_RANK), rank, 0.0))
    h2_ref[0, :, D_MODEL:] = route
    meta_ref[0, 0] = route.T[0:8, :]


def _outproj(x, ohf, ohb, odf, odb, p, mod, hg_nw, dn_nw, w_out, norm2_w, w_route, b_route):
    nbatch, lt, d = x.shape
    grid = (nbatch, lt // ROWS)
    xspec = pl.BlockSpec((1, ROWS, d), lambda b, i: (b, i, 0))
    ospec = pl.BlockSpec((1, ROWS, GROUP_W), lambda b, i: (b, i, 0))

    def pcol(cb):
        return pl.BlockSpec((1, ROWS, GROUP_W), lambda b, i: (b, i, cb))

    def full(shape):
        return pl.BlockSpec(shape, lambda b, i: tuple(0 for _ in shape))

    return pl.pallas_call(
        _outproj_kernel,
        out_shape=(jax.ShapeDtypeStruct(x.shape, F32), jax.ShapeDtypeStruct((nbatch, lt, d + LANES), F32),
                   jax.ShapeDtypeStruct((nbatch, lt // ROWS, 8, ROWS), F32),
                   jax.ShapeDtypeStruct((8, LANES), F32)),
        grid=grid,
        in_specs=[xspec, ospec, ospec, ospec, ospec, pcol(HG_G), pcol(DN_Z),
                  pl.BlockSpec((1, 6, d), _mod_row(nbatch)),
                  full((1, HEAD_DIM)), full((1, HEAD_DIM)), full((d, d)), full((1, d)),
                  full((d, LANES)), full((1, LANES))],
        out_specs=(xspec, pl.BlockSpec((1, ROWS, d + LANES), lambda b, i: (b, i, 0)),
                   pl.BlockSpec((1, 1, 8, ROWS), lambda b, i: (b, i, 0, 0)), full((8, LANES))),
        scratch_shapes=[pltpu.VMEM((8, LANES), F32)],
        compiler_params=_params("arbitrary", "arbitrary"), name="out_proj",
    )(x, ohf, ohb, odf, odb, p, p, mod, hg_nw, dn_nw, w_out, norm2_w, w_route, b_route)


EXPERT_TILE = 256
MOVE_TILE = 1024


def _plan(cls, rank, counts):
    t = cls.shape[0]
    tiles_c = (counts + EXPERT_TILE - 1) // EXPERT_TILE
    tile_end = jnp.cumsum(tiles_c)
    tile_start = tile_end - tiles_c
    pos = tile_start[cls] * EXPERT_TILE + rank
    ntiles = t // EXPERT_TILE + N_CLASSES
    tile_id = jnp.arange(ntiles, dtype=jnp.int32)
    tile_cls = jnp.minimum(jnp.sum((tile_id[:, None] >= tile_end[None, :]).astype(jnp.int32), axis=1),
                           N_CLASSES - 1)
    used = (tile_id < tile_end[-1]).astype(jnp.int32)
    pair_a = jnp.array([a for a, _ in PAIRS], jnp.int32)
    pair_b = jnp.array([b for _, b in PAIRS], jnp.int32)
    group = tile_cls // len(PAIRS)
    ea = group * EXPERTS_PER_GROUP + pair_a[tile_cls % len(PAIRS)]
    eb = group * EXPERTS_PER_GROUP + pair_b[tile_cls % len(PAIRS)]
    return pos, ntiles, ea, eb, used


def _scatter_kernel(idx_ref, x_ref, init_ref, dst_ref, sem):
    rows = x_ref.shape[0]

    def issue(r, carry):
        pltpu.make_async_copy(x_ref.at[pl.ds(r, 1), :], dst_ref.at[pl.ds(idx_ref[0, 0, r], 1), :], sem).start()
        return carry

    lax.fori_loop(0, rows, issue, 0, unroll=8)
    pltpu.make_async_copy(x_ref, dst_ref.at[pl.ds(0, rows), :], sem).wait()


def _gather_kernel(idx_ref, src_ref, o_ref, sem):
    rows = o_ref.shape[0]

    def issue(r, carry):
        pltpu.make_async_copy(src_ref.at[pl.ds(idx_ref[0, 0, r], 1), :], o_ref.at[pl.ds(r, 1), :], sem).start()
        return carry

    lax.fori_loop(0, rows, issue, 0, unroll=8)
    pltpu.make_async_copy(src_ref.at[pl.ds(0, rows), :], o_ref, sem).wait()


def _move_rows(src, idx, n_out, scatter):
    n = idx.shape[0]
    d = src.shape[1]
    tm = max(m for m in range(ROWS, MOVE_TILE + 1, ROWS) if n % m == 0)
    idx_spec = pl.BlockSpec((1, 1, tm), lambda i: (i, 0, 0), memory_space=pltpu.SMEM)
    hbm = pl.BlockSpec(memory_space=pl.ANY)
    blk = pl.BlockSpec((tm, d), lambda i: (i, 0))
    common = dict(out_shape=jax.ShapeDtypeStruct((n_out, d), src.dtype), grid=(n // tm,),
                  scratch_shapes=[pltpu.SemaphoreType.DMA], compiler_params=_params("arbitrary"))
    idx3 = idx.reshape(n // tm, 1, tm)
    if scatter:
        return pl.pallas_call(_scatter_kernel, in_specs=[idx_spec, blk, hbm], out_specs=hbm,
                              input_output_aliases={2: 0}, name="scatter_rows", **common
                              )(idx3, src, jnp.zeros((n_out, d), src.dtype))
    return pl.pallas_call(_gather_kernel, in_specs=[idx_spec, hbm], out_specs=blk,
                          name="gather_rows", **common)(idx3, src)


def _expert_kernel(ea_ref, eb_ref, used_ref, x_ref, wga_ref, wua_ref, wda_ref, wgb_ref, wub_ref, wdb_ref, o_ref):
    i = pl.program_id(0)

    @pl.when(used_ref[i] != 0)
    def _():
        x = x_ref[:, :D_MODEL].astype(BF16)
        gate_a = x_ref[:, D_MODEL + ROUTE_GATE_A:D_MODEL + ROUTE_GATE_A + 1]
        gate_b = x_ref[:, D_MODEL + ROUTE_GATE_B:D_MODEL + ROUTE_GATE_B + 1]
        act_a = _silu(_dot(x, wga_ref[0])) * _dot(x, wua_ref[0]) * gate_a
        act_b = _silu(_dot(x, wgb_ref[0])) * _dot(x, wub_ref[0]) * gate_b
        o_ref[...] = _dot(act_a.astype(BF16), wda_ref[0]) + _dot(act_b.astype(BF16), wdb_ref[0])

    @pl.when(used_ref[i] == 0)
    def _():
        o_ref[...] = jnp.zeros_like(o_ref)


def _experts(xs_sorted, ea, eb, used, w_gate, w_up, w_down):
    n, dx = xs_sorted.shape
    _, d, ff = w_gate.shape
    tm = EXPERT_TILE
    wa = lambda i, ea, eb, used: (ea[i], 0, 0)
    wb = lambda i, ea, eb, used: (eb[i], 0, 0)
    row = lambda i, ea, eb, used: (i, 0)
    return pl.pallas_call(
        _expert_kernel,
        out_shape=jax.ShapeDtypeStruct((n, d), F32),
        grid_spec=pltpu.PrefetchScalarGridSpec(
            num_scalar_prefetch=3,
            grid=(n // tm,),
            in_specs=[pl.BlockSpec((tm, dx), row),
                      pl.BlockSpec((1, d, ff), wa), pl.BlockSpec((1, d, ff), wa), pl.BlockSpec((1, ff, d), wa),
                      pl.BlockSpec((1, d, ff), wb), pl.BlockSpec((1, d, ff), wb), pl.BlockSpec((1, ff, d), wb)],
            out_specs=pl.BlockSpec((tm, d), row)),
        compiler_params=_params("arbitrary"), name="moe_experts",
    )(ea, eb, used, xs_sorted, w_gate, w_up, w_down, w_gate, w_up, w_down)


def _moe(h2ext, meta, counts, w_gate, w_up, w_down):
    cls = meta[:, :, ROUTE_CLS, :].reshape(-1).astype(jnp.int32)
    rank = meta[:, :, ROUTE_RANK, :].reshape(-1).astype(jnp.int32)
    pos, ntiles, ea, eb, used = _plan(cls, rank, counts)
    xs_sorted = _move_rows(h2ext, pos, ntiles * EXPERT_TILE, scatter=True)
    ys_sorted = _experts(xs_sorted, ea, eb, used, w_gate, w_up, w_down)
    return _move_rows(ys_sorted, pos, h2ext.shape[0], scatter=False)


def _route(lg):
    lane = lax.broadcasted_iota(jnp.int32, lg.shape, 1).astype(F32)
    neg = -jnp.inf
    big = float(LANES)

    def first_argmax(vals):
        m = jnp.max(vals, axis=-1, keepdims=True)
        return m, jnp.min(jnp.where(vals == m, lane, big), axis=-1, keepdims=True)

    in_groups = lane < N_GROUPS
    gl = jnp.where(in_groups, lg, neg)
    gmax, gidx = first_argmax(gl)
    g_w = 1.0 / jnp.sum(jnp.where(in_groups, jnp.exp(gl - gmax), 0.0), axis=-1, keepdims=True)
    lo = ROUTE_OFF + EXPERTS_PER_GROUP * gidx
    el = jnp.where(jnp.logical_and(lane >= lo, lane < lo + EXPERTS_PER_GROUP), lg, neg)
    m1, i1 = first_argmax(el)
    el2 = jnp.where(lane == i1, neg, el)
    m2, i2 = first_argmax(el2)
    e2 = jnp.exp(m2 - m1)
    w1 = g_w / (1.0 + e2)
    w2 = g_w * e2 / (1.0 + e2)
    first_lower = i1 < i2
    a = jnp.minimum(i1, i2) - lo
    b = jnp.maximum(i1, i2) - lo
    pair = a * (EXPERTS_PER_GROUP - 1) - a * (a - 1.0) * 0.5 + (b - a - 1.0)
    cls = gidx * float(len(PAIRS)) + pair
    return cls, jnp.where(first_lower, w1, w2), jnp.where(first_lower, w2, w1)


def _final_kernel(x_ref, y_ref, m_ref, nw_ref, o_ref):
    x = x_ref[0] + m_ref[0, 5:6, :] * y_ref[0]
    o_ref[0] = _rms(x, nw_ref[...])


def _final(x, y, mod, norm_w, ctx_blocks):
    nbatch, lt, d = x.shape
    nlat = lt // ROWS - ctx_blocks
    xspec = pl.BlockSpec((1, ROWS, d), lambda b, i: (b, i + ctx_blocks, 0))
    return pl.pallas_call(
        _final_kernel,
        out_shape=jax.ShapeDtypeStruct((nbatch, nlat * ROWS, d), F32),
        grid=(nbatch, nlat),
        in_specs=[xspec, xspec, pl.BlockSpec((1, 6, d), lambda b, i: (b, 0, 0)),
                  pl.BlockSpec((1, d), lambda b, i: (0, 0))],
        out_specs=pl.BlockSpec((1, ROWS, d), lambda b, i: (b, i, 0)),
        compiler_params=_params("arbitrary", "arbitrary"), name="final_norm",
    )(x, y, mod, norm_w)


def kernel(x, c, ctx, c_ctx, w_mod, b_mod, norm1_w, norm2_w, w_in, w_out, hg_lb, hg_norm_w, dn_conv_w,
           dn_a_log, dn_dt_bias, dn_norm_w, w_group, b_group, w_expert, b_expert, w_gate, w_up, w_down,
           final_norm_w):
    nbatch, seq, d = x.shape
    ctx_len = ctx.shape[1]
    depth = w_mod.shape[0]
    assert d == D_MODEL and ctx_len == ROWS and seq % ROWS == 0 and ROWS % GRID_W == 0
    assert w_in.shape[2] == IN_COLS
    lt = ctx_len + seq

    nb = -(-(nbatch + 1) // 8) * 8
    cvec = jnp.concatenate([c, c_ctx[None, :], jnp.zeros((nb - nbatch - 1, d), F32)], axis=0)
    mods = _modulation(cvec, w_mod, b_mod).reshape(depth, nb, 6, d)
    lbs = _lower_bounds(hg_lb)

    c_dn = 5 * GROUP_W
    w_qkv = w_in[:, :, c_dn:c_dn + 3 * GROUP_W].astype(BF16)
    w_rest = jnp.concatenate([w_in[:, :, :c_dn], w_in[:, :, c_dn + 3 * GROUP_W:],
                              jnp.zeros((depth, d, REST_COLS + 3 * GROUP_W - IN_COLS), F32)], axis=2).astype(BF16)
    w_out_b = w_out.astype(BF16)
    w_gate_b, w_up_b, w_down_b = w_gate.astype(BF16), w_up.astype(BF16), w_down.astype(BF16)
    nroute = N_GROUPS + N_EXPERTS
    w_route = jnp.pad(jnp.concatenate([w_group, w_expert], axis=2), ((0, 0), (0, 0), (0, LANES - nroute)))
    b_route = jnp.pad(jnp.concatenate([b_group, b_expert], axis=1), ((0, 0), (0, LANES - nroute)))
    conv_w = dn_conv_w.reshape(depth, 9, 3 * GROUP_W)
    a_flat = dn_a_log.reshape(depth, 2 * N_HEADS)
    dt_flat = dn_dt_bias.reshape(depth, 2 * N_HEADS)
    lane_vals = lambda v: jnp.pad(v, ((0, 0), (2 * N_HEADS, LANES - 4 * N_HEADS)))
    acoef_c = jnp.pad(jnp.stack([lane_vals(a_flat), lane_vals(dt_flat)], axis=1), ((0, 0), (0, 6), (0, 0)))

    xs = jnp.concatenate([ctx, x], axis=1)
    y = None
    for l in range(depth):
        xs, qkv, p = _inproj(xs, y, mods[l - 1] if l else None, mods[l], norm1_w[l][None, :],
                             w_qkv[l], w_rest[l], conv_w[l])
        ohf, ohb, odf, odb = _scans(p, qkv, lbs[l], acoef_c[l])
        xs, h2ext, meta, counts = _outproj(xs, ohf, ohb, odf, odb, p, mods[l], hg_norm_w[l][None, :],
                                           dn_norm_w[l][None, :], w_out_b[l], norm2_w[l][None, :],
                                           w_route[l], b_route[l][None, :])
        y = _moe(h2ext.reshape(nbatch * lt, d + LANES), meta, counts[0, :N_CLASSES].astype(jnp.int32),
                 w_gate_b[l], w_up_b[l], w_down_b[l]).reshape(nbatch, lt, d)
    return _final(xs, y, mods[depth - 1], final_norm_w[None, :], ctx_len // ROWS)
```
